```python
import math
import jax, jax.numpy as jnp
from jax import lax
import numpy as np

D_MODEL = 2048
BATCH = 4
SEQ = 4096
DEPTH = 2

ATT_HEAD_DIM = 64
ATT_SLOTS = 8
DILATION_PAIRS = ((128, 1), (512, 4), (2048, 16))
N_DIL = len(DILATION_PAIRS)
ATT_HEADS = ATT_SLOTS * N_DIL
ATT_WIDTH = ATT_HEADS * ATT_HEAD_DIM
ATT_OUT = ATT_SLOTS * ATT_HEAD_DIM
SSD_HEAD_DIM = 64
SSD_INNER = D_MODEL // 2
SSD_HEADS = SSD_INNER // SSD_HEAD_DIM
SSD_GROUPS = 2
SSD_HPG = SSD_HEADS // SSD_GROUPS
SSD_STATE = 128
SSD_CONV = 5
SSD_CHUNK = 128
CONV_CH = SSD_INNER + 2 * SSD_GROUPS * SSD_STATE
POOL_WINDOWS = (2, 4, 8, 16)
POOL_GROUP = D_MODEL // 16
POOL_WIDTH = POOL_GROUP * len(POOL_WINDOWS)
MIX_WIDTH = ATT_OUT + SSD_INNER + POOL_WIDTH
IN_SIZES = (ATT_WIDTH, ATT_WIDTH, ATT_WIDTH, SSD_INNER, CONV_CH, 2 * SSD_HEADS, POOL_WIDTH)
IN_WIDTH = sum(IN_SIZES)
D_FF = -(-8 * D_MODEL // (3 * 256)) * 256
RMS_EPS = 1e-6
NEG_INF = -1e30

kernel_name = "hybrid_dilated_ssd_pool_encoder"


def rms_norm(x, w):
    xf = x.astype(jnp.float32)
    y = xf * lax.rsqrt(jnp.mean(xf * xf, axis=-1, keepdims=True) + RMS_EPS)
    return (y * w.astype(jnp.float32)).astype(x.dtype)


def alibi_slopes():
    k = jnp.arange(1, ATT_HEADS + 1, dtype=jnp.float32)
    return (2.0 ** (-8.0 * k / ATT_HEADS)).reshape(N_DIL, ATT_SLOTS)


def dilated_window_attention(q, k, v, slopes, window, dilation):
    b, s, h, e = q.shape
    half = window // (2 * dilation)
    L = s // dilation
    nb = -(-L // half)
    Lp = nb * half

    def split(t):
        t = t.reshape(b, L, dilation, h, e).transpose(0, 2, 3, 1, 4)
        return jnp.pad(t, ((0, 0), (0, 0), (0, 0), (0, Lp - L), (0, 0)))

    def band(t):
        tp = jnp.pad(t, ((0, 0), (0, 0), (0, 0), (half, half), (0, 0))).reshape(b, dilation, h, nb + 2, half, e)
        return jnp.concatenate([tp[:, :, :, :-2], tp[:, :, :, 1:-1], tp[:, :, :, 2:]], axis=-2)

    qb = split(q).reshape(b, dilation, h, nb, half, e)
    kb = band(split(k))
    vb = band(split(v))
    scores = jnp.einsum('bdhiqe,bdhike->bdhiqk', qb, kb) * (e ** -0.5)
    qpos = jnp.arange(nb)[:, None] * half + jnp.arange(half)[None, :]
    kpos = jnp.arange(nb)[:, None] * half - half + jnp.arange(3 * half)[None, :]
    rel = kpos[:, None, :] - qpos[:, :, None]
    valid = (jnp.abs(rel) <= half) & (kpos[:, None, :] >= 0) & (kpos[:, None, :] < L)
    dist = (jnp.abs(rel) * dilation).astype(jnp.float32)
    scores = scores - slopes[:, None, None, None] * dist[None]
    scores = jnp.where(valid, scores, NEG_INF)
    m = jnp.max(scores, axis=-1, keepdims=True)
    p = jnp.exp(scores - m)
    l = jnp.sum(p, axis=-1, keepdims=True)
    o = jnp.einsum('bdhiqk,bdhike->bdhiqe', p, vb) / l
    lse = m[..., 0] + jnp.log(l[..., 0])
    o = o.reshape(b, dilation, h, Lp, e)[:, :, :, :L].transpose(0, 3, 1, 2, 4).reshape(b, s, h, e)
    lse = lse.reshape(b, dilation, h, Lp)[:, :, :, :L].transpose(0, 3, 1, 2).reshape(b, s, h)
    return o, lse


def ssd_scan(x, dt, a, bm, cm):
    bsz, s, g, h, p = x.shape
    n = bm.shape[-1]
    q = SSD_CHUNK
    c = s // q
    x = x.reshape(bsz, c, q, g, h, p)
    dt = dt.reshape(bsz, c, q, g, h)
    bm = bm.reshape(bsz, c, q, g, n)
    cm = cm.reshape(bsz, c, q, g, n)
    a_cs = jnp.cumsum((dt * a).transpose(0, 3, 4, 1, 2), axis=-1)
    lower = jnp.tril(jnp.ones((q, q), dtype=bool))
    diff = a_cs[..., :, None] - a_cs[..., None, :]
    seg = jnp.where(lower, jnp.exp(jnp.minimum(diff, 0.0)), 0.0)
    xdt = x * dt[..., None]
    cb = jnp.einsum('bclgn,bcsgn->bgcls', cm, bm)
    y_diag = jnp.einsum('bgcls,bghcls,bcsghp->bclghp', cb, seg, xdt)
    decay_states = jnp.exp(a_cs[..., -1:] - a_cs)
    states = jnp.einsum('bclgn,bghcl,bclghp->cbghpn', bm, decay_states, xdt)
    chunk_decay = jnp.exp(a_cs[..., -1]).transpose(3, 0, 1, 2)

    def step(carry, inp):
        st, dec = inp
        return carry * dec[..., None, None] + st, carry

    init = jnp.zeros((bsz, g, h, p, n), dtype=x.dtype)
    _, prev = lax.scan(step, init, (states, chunk_decay))
    y_off = jnp.einsum('bclgn,cbghpn,bghcl->bclghp', cm, prev, jnp.exp(a_cs))
    return (y_diag + y_off).reshape(bsz, s, g, h, p)


def ssd_mixer(z, xbc_raw, dt_raw, conv_w, conv_b, dt_bias, a_log, d_skip, norm_w):
    b, s, _ = z.shape
    pad = SSD_CONV // 2
    xbc = lax.conv_general_dilated(xbc_raw, conv_w.astype(jnp.float32)[:, None, :], window_strides=(1,),
                                   padding=[(pad, pad)], dimension_numbers=('NWC', 'WIO', 'NWC'),
                                   feature_group_count=CONV_CH)
    xbc = jax.nn.silu(xbc + conv_b.astype(jnp.float32))
    xs = xbc[..., :SSD_INNER].reshape(b, s, SSD_GROUPS, SSD_HPG, SSD_HEAD_DIM)
    gn = SSD_GROUPS * SSD_STATE
    bm = xbc[..., SSD_INNER:SSD_INNER + gn].reshape(b, s, SSD_GROUPS, SSD_STATE)
    cm = xbc[..., SSD_INNER + gn:].reshape(b, s, SSD_GROUPS, SSD_STATE)
    a = -jnp.exp(a_log.astype(jnp.float32)).reshape(2, SSD_GROUPS, SSD_HPG)
    dt = jax.nn.softplus(dt_raw.reshape(b, s, 2, SSD_GROUPS, SSD_HPG)
                         + dt_bias.astype(jnp.float32).reshape(2, SSD_GROUPS, SSD_HPG))
    flip = lambda t: t[:, ::-1]
    y_fwd = ssd_scan(xs, dt[:, :, 0], a[0], bm, cm)
    y_bwd = flip(ssd_scan(flip(xs), flip(dt[:, :, 1]), a[1], flip(bm), flip(cm)))
    y = y_fwd + y_bwd + xs * d_skip.astype(jnp.float32).reshape(SSD_GROUPS, SSD_HPG)[..., None]
    y = y.reshape(b, s, SSD_INNER) * jax.nn.silu(z)
    yg = y.reshape(b, s, SSD_GROUPS, SSD_INNER // SSD_GROUPS)
    yg = yg * lax.rsqrt(jnp.mean(yg * yg, axis=-1, keepdims=True) + RMS_EPS)
    return yg.reshape(b, s, SSD_INNER) * norm_w.astype(jnp.float32)


def multiscale_pool(u, pool_w, pool_scale):
    b, s, _ = u.shape
    cs = jnp.concatenate([jnp.zeros((b, 1, POOL_WIDTH), u.dtype), jnp.cumsum(u, axis=1)], axis=1)
    t = jnp.arange(s)
    outs = []
    for g, w in enumerate(POOL_WINDOWS):
        sl = slice(g * POOL_GROUP, (g + 1) * POOL_GROUP)
        lo = jnp.clip(t - w // 2, 0, s)
        hi = jnp.clip(t + w // 2, 0, s)
        cnt = (hi - lo).astype(jnp.float32)
        mean = (cs[:, hi, sl] - cs[:, lo, sl]) / cnt[None, :, None]
        outs.append(jnp.einsum('bsc,cd->bsd', mean - u[..., sl], pool_w[g].astype(jnp.float32)))
    return jnp.concatenate(outs, axis=-1) * pool_scale.astype(jnp.float32)


def hybrid_layer(x, norm1_w, w_in, conv_w, conv_b, dt_bias, a_log, d_skip, ssd_norm_w,
                 pool_w, pool_scale, w_out, norm2_w, w_gate, w_up, w_down):
    b, s, _ = x.shape
    h = rms_norm(x, norm1_w)
    proj = jnp.matmul(h, w_in).astype(jnp.float32)
    cuts = [int(c) for c in np.cumsum(IN_SIZES)[:-1]]
    q, k, v, z, xbc, dt_raw, u = jnp.split(proj, cuts, axis=-1)
    q = q.reshape(b, s, N_DIL, ATT_SLOTS, ATT_HEAD_DIM)
    k = k.reshape(b, s, N_DIL, ATT_SLOTS, ATT_HEAD_DIM)
    v = v.reshape(b, s, N_DIL, ATT_SLOTS, ATT_HEAD_DIM)
    slopes = alibi_slopes()
    outs, lses = [], []
    for gi, (window, dilation) in enumerate(DILATION_PAIRS):
        o, l = dilated_window_attention(q[:, :, gi], k[:, :, gi], v[:, :, gi], slopes[gi], window, dilation)
        outs.append(o)
        lses.append(l)
    wts = jax.nn.softmax(jnp.stack(lses, axis=-1), axis=-1)
    att = jnp.sum(jnp.stack(outs, axis=-1) * wts[:, :, :, None, :], axis=-1).reshape(b, s, ATT_OUT)
    ssd = ssd_mixer(z, xbc, dt_raw, conv_w, conv_b, dt_bias, a_log, d_skip, ssd_norm_w)
    pool = multiscale_pool(u, pool_w, pool_scale)
    mix = jnp.concatenate([att, ssd, pool], axis=-1).astype(x.dtype)
    x = x + jnp.matmul(mix, w_out)
    h = rms_norm(x, norm2_w)
    ff = jax.nn.silu(jnp.matmul(h, w_gate)) * jnp.matmul(h, w_up)
    return x + jnp.matmul(ff, w_down)


def setup_inputs(seed: int = 0) -> dict:
    key = jax.random.key(seed)
    ks = jax.random.split(key, 20)
    f32 = jnp.float32
    nrm = lambda k, shape, scale: jax.random.normal(k, shape, f32) * scale
    dt0 = jnp.exp(jax.random.uniform(ks[6], (DEPTH, 2, SSD_HEADS), f32) * (math.log(0.1) - math.log(0.001))
                  + math.log(0.001))
    return {
        "x": nrm(ks[0], (BATCH, SEQ, D_MODEL), 1.0),
        "norm1_w": 1.0 + nrm(ks[1], (DEPTH, D_MODEL), 0.02),
        "w_in": nrm(ks[2], (DEPTH, D_MODEL, IN_WIDTH), D_MODEL ** -0.5),
        "conv_w": nrm(ks[3], (DEPTH, SSD_CONV, CONV_CH), SSD_CONV ** -0.5),
        "conv_b": nrm(ks[4], (DEPTH, CONV_CH), 0.02),
        "dt_bias": dt0 + jnp.log(-jnp.expm1(-dt0)),
        "a_log": jnp.log(jax.random.uniform(ks[7], (DEPTH, 2, SSD_HEADS), f32, 1.0, 16.0)),
        "d_skip": 1.0 + nrm(ks[8], (DEPTH, SSD_HEADS), 0.02),
        "ssd_norm_w": 1.0 + nrm(ks[9], (DEPTH, SSD_INNER), 0.02),
        "pool_w": nrm(ks[10], (DEPTH, len(POOL_WINDOWS), POOL_GROUP, POOL_GROUP), POOL_GROUP ** -0.5),
        "pool_scale": 1.0 + nrm(ks[11], (DEPTH, POOL_WIDTH), 0.02),
        "w_out": nrm(ks[12], (DEPTH, MIX_WIDTH, D_MODEL), MIX_WIDTH ** -0.5),
        "norm2_w": 1.0 + nrm(ks[13], (DEPTH, D_MODEL), 0.02),
        "w_gate": nrm(ks[14], (DEPTH, D_MODEL, D_FF), D_MODEL ** -0.5),
        "w_up": nrm(ks[15], (DEPTH, D_MODEL, D_FF), D_MODEL ** -0.5),
        "w_down": nrm(ks[16], (DEPTH, D_FF, D_MODEL), D_FF ** -0.5),
        "final_norm_w": 1.0 + nrm(ks[17], (D_MODEL,), 0.02),
    }


def reference(x, norm1_w, w_in, conv_w, conv_b, dt_bias, a_log, d_skip, ssd_norm_w, pool_w, pool_scale,
              w_out, norm2_w, w_gate, w_up, w_down, final_norm_w):
    for l in range(DEPTH):
        x = hybrid_layer(x, norm1_w[l], w_in[l], conv_w[l], conv_b[l], dt_bias[l], a_log[l], d_skip[l],
                         ssd_norm_w[l], pool_w[l], pool_scale[l], w_out[l], norm2_w[l],
                         w_gate[l], w_up[l], w_down[l])
    return rms_norm(x, final_norm_w)
```

```python
import functools
import math

import jax
import jax.numpy as jnp
from jax import lax
from jax.experimental import pallas as pl
from jax.experimental.pallas import tpu as pltpu

F32 = jnp.float32
BF16 = jnp.bfloat16

D_MODEL = 2048
ATT_HEAD_DIM = 64
ATT_SLOTS = 8
DILATIONS = (1, 4, 16)
ATT_HALF = 64
N_DIL = len(DILATIONS)
ATT_HEADS = ATT_SLOTS * N_DIL
ATT_WIDTH = ATT_HEADS * ATT_HEAD_DIM
ATT_OUT = ATT_SLOTS * ATT_HEAD_DIM
SSD_HEAD_DIM = 64
SSD_INNER = D_MODEL // 2
SSD_HEADS = SSD_INNER // SSD_HEAD_DIM
SSD_GROUPS = 2
SSD_HPG = SSD_HEADS // SSD_GROUPS
SSD_STATE = 128
SSD_CONV = 5
SSD_CHUNK = 128
CONV_CH = SSD_INNER + 2 * SSD_GROUPS * SSD_STATE
POOL_WINDOWS = (2, 4, 8, 16)
POOL_GROUP = 128
POOL_WIDTH = POOL_GROUP * len(POOL_WINDOWS)
MIX_WIDTH = ATT_OUT + SSD_INNER + POOL_WIDTH
D_FF = 5632
RMS_EPS = 1e-6
NEG_INF = -1e30

LANES = 128
IN_TILE = 1536
N_IN_TILES = 5
VMEM_LIMIT = 56 * 1024 * 1024


def _cparams(sem):
    return pltpu.CompilerParams(dimension_semantics=sem, vmem_limit_bytes=VMEM_LIMIT)


def _silu(x):
    return x * (1.0 / (1.0 + jnp.exp(-x)))


def _softplus(x):
    return jnp.maximum(x, 0.0) + jnp.log(1.0 + jnp.exp(-jnp.abs(x)))


def _inproj_kernel(x_ref, nw_ref, w_ref, wdt_ref, q0_ref, q1_ref, q2_ref, zux_ref, dt_ref,
                   h_ref, stage_ref, *, tm):
    j = pl.program_id(1)

    @pl.when(j == 0)
    def _():
        x = x_ref[...]
        ms = jnp.mean(x * x, axis=-1, keepdims=True)
        h = x * lax.rsqrt(ms + RMS_EPS) * nw_ref[...]
        h_ref[...] = h.astype(BF16)
        dt_ref[...] = jnp.dot(h_ref[...], wdt_ref[...], preferred_element_type=F32)

    def proj():
        return jnp.dot(h_ref[...], w_ref[...], preferred_element_type=F32)

    @pl.when(j == 0)
    def _():
        q0_ref[0, 0] = proj().astype(BF16)

    for g, (d, out_ref) in enumerate(zip(DILATIONS, (q0_ref, q1_ref, q2_ref))):
        if g == 0:
            continue

        @pl.when(j == g)
        def _(d=d, out_ref=out_ref):
            res = proj()
            for cb in range(IN_TILE // LANES):
                stage_ref[cb] = res[:, cb * LANES:(cb + 1) * LANES]
            for r in range(d):
                for cb in range(IN_TILE // LANES):
                    out_ref[0, r, :, cb * LANES:(cb + 1) * LANES] = (
                        stage_ref[cb, pl.ds(r, tm // d, stride=d), :].astype(BF16))

    @pl.when(j >= N_DIL)
    def _():
        zux_ref[...] = proj()


def _inproj(x2, nw, wcat, wdt, *, batch, seq, tm):
    t = x2.shape[0]
    per_b = seq // tm
    grid = (t // tm, N_IN_TILES)

    def qspec(d):
        return pl.BlockSpec((1, d, tm // d, IN_TILE), lambda i, j: (i // per_b, 0, i % per_b, 0))

    out_shape = [jax.ShapeDtypeStruct((batch, d, seq // d, IN_TILE), BF16) for d in DILATIONS]
    out_shape += [jax.ShapeDtypeStruct((t, 2 * IN_TILE), F32), jax.ShapeDtypeStruct((t, LANES), F32)]
    return pl.pallas_call(
        functools.partial(_inproj_kernel, tm=tm),
        grid=grid,
        in_specs=[
            pl.BlockSpec((tm, D_MODEL), lambda i, j: (i, 0)),
            pl.BlockSpec((1, D_MODEL), lambda i, j: (0, 0)),
            pl.BlockSpec((D_MODEL, IN_TILE), lambda i, j: (0, j)),
            pl.BlockSpec((D_MODEL, LANES), lambda i, j: (0, 0)),
        ],
        out_specs=[qspec(d) for d in DILATIONS] + [
            pl.BlockSpec((tm, IN_TILE), lambda i, j: (i, jnp.maximum(j - N_DIL, 0))),
            pl.BlockSpec((tm, LANES), lambda i, j: (i, 0)),
        ],
        out_shape=out_shape,
        scratch_shapes=[pltpu.VMEM((tm, D_MODEL), BF16), pltpu.VMEM((IN_TILE // LANES, tm, LANES), F32)],
        compiler_params=_cparams(("arbitrary", "arbitrary")),
        name="inproj",
    )(x2, nw, wcat, wdt)


ATT_QB = 128
ATT_KW = 256


def _attn_kernel(slopes_ref, q_ref, k_ref, v_ref, o_ref, lse_ref, *, seq_len, dil):
    sp = pl.program_id(2)
    slope_a = slopes_ref[2 * sp] * float(dil)
    slope_b = slopes_ref[2 * sp + 1] * float(dil)
    rows = lax.broadcasted_iota(jnp.int32, (2 * ATT_QB, ATT_KW), 0)
    cols = lax.broadcasted_iota(jnp.int32, (2 * ATT_QB, ATT_KW), 1)
    rel_base = cols - (rows & (ATT_QB - 1))
    slope = jnp.where(rows < ATT_QB, slope_a, slope_b)
    is_a = lax.broadcasted_iota(jnp.int32, (ATT_QB, LANES), 1) < ATT_HEAD_DIM

    def body(i, carry):
        q0 = pl.multiple_of(i * ATT_QB, ATT_QB)
        k0 = pl.multiple_of(jnp.clip(i * ATT_QB - ATT_HALF, 0, seq_len - ATT_KW), ATT_HALF)
        q2 = q_ref[0, 0, pl.ds(q0, ATT_QB), :] * jnp.asarray(ATT_HEAD_DIM ** -0.5, BF16)
        zero = jnp.zeros_like(q2)
        qs = jnp.concatenate([jnp.where(is_a, q2, zero), jnp.where(is_a, zero, q2)], axis=0)
        kk = k_ref[0, 0, pl.ds(k0, ATT_KW), :]
        vv = v_ref[0, 0, pl.ds(k0, ATT_KW), :]
        s = lax.dot_general(qs, kk, (((1,), (1,)), ((), ())), preferred_element_type=F32)
        dist = jnp.abs(rel_base + (k0 - q0))
        s = jnp.where(dist <= ATT_HALF, s - slope * dist.astype(F32), NEG_INF)
        m = jnp.max(s, axis=-1, keepdims=True)
        p = jnp.exp(s - m)
        l = jnp.sum(p, axis=-1, keepdims=True)
        pv = jnp.dot(p.astype(BF16), vv, preferred_element_type=F32)
        o = pv / l
        lse = jnp.broadcast_to(m + jnp.log(l), (2 * ATT_QB, LANES))
        o_ref[0, 0, pl.ds(q0, ATT_QB), :] = jnp.where(is_a, o[:ATT_QB], o[ATT_QB:])
        lse_ref[0, 0, pl.ds(q0, ATT_QB), :] = jnp.where(is_a, lse[:ATT_QB], lse[ATT_QB:])
        return carry

    lax.fori_loop(0, seq_len // ATT_QB, body, 0)


def _attention_group(qkv, slopes, *, dil):
    batch, d, seq_len, _ = qkv.shape
    n_pairs = ATT_OUT // LANES

    def spec(col0):
        return pl.BlockSpec((1, 1, seq_len, LANES), lambda b, r, sp: (b, r, 0, col0 + sp))

    out = jax.ShapeDtypeStruct((batch, d, seq_len, ATT_OUT), F32)
    return pl.pallas_call(
        functools.partial(_attn_kernel, seq_len=seq_len, dil=dil),
        grid=(batch, d, n_pairs),
        in_specs=[pl.BlockSpec(memory_space=pltpu.SMEM), spec(0), spec(n_pairs), spec(2 * n_pairs)],
        out_specs=[spec(0), spec(0)],
        out_shape=[out, out],
        compiler_params=_cparams(("arbitrary", "arbitrary", "arbitrary")),
        name=f"attn_d{dil}",
    )(slopes, qkv, qkv, qkv)


def _merge_kernel(o0_ref, l0_ref, o1_ref, l1_ref, o2_ref, l2_ref, att_ref, so_ref, sl_ref, *, tm):
    outs = [o0_ref[0, 0]]
    lses = [l0_ref[0, 0]]
    for d, o_ref, l_ref in ((DILATIONS[1], o1_ref, l1_ref), (DILATIONS[2], o2_ref, l2_ref)):
        for r in range(d):
            for cb in range(ATT_OUT // LANES):
                cols = slice(cb * LANES, (cb + 1) * LANES)
                so_ref[cb, pl.ds(r, tm // d, stride=d), :] = o_ref[0, r, :, cols]
                sl_ref[cb, pl.ds(r, tm // d, stride=d), :] = l_ref[0, r, :, cols]
        outs.append(jnp.concatenate([so_ref[cb] for cb in range(ATT_OUT // LANES)], axis=1))
        lses.append(jnp.concatenate([sl_ref[cb] for cb in range(ATT_OUT // LANES)], axis=1))
    mx = jnp.maximum(jnp.maximum(lses[0], lses[1]), lses[2])
    ws = [jnp.exp(l - mx) for l in lses]
    num = ws[0] * outs[0] + ws[1] * outs[1] + ws[2] * outs[2]
    att_ref[...] = (num / (ws[0] + ws[1] + ws[2])).astype(BF16)


def _merge(ol, *, batch, seq, tm):
    t = batch * seq
    per_b = seq // tm
    in_specs = []
    args = []
    for d, (o, l) in zip(DILATIONS, ol):
        spec = pl.BlockSpec((1, d, tm // d, ATT_OUT), lambda i: (i // per_b, 0, i % per_b, 0))
        in_specs += [spec, spec]
        args += [o, l]
    return pl.pallas_call(
        functools.partial(_merge_kernel, tm=tm),
        grid=(t // tm,),
        in_specs=in_specs,
        out_specs=pl.BlockSpec((tm, ATT_OUT), lambda i: (i, 0)),
        out_shape=jax.ShapeDtypeStruct((t, ATT_OUT), BF16),
        scratch_shapes=[pltpu.VMEM((ATT_OUT // LANES, tm, LANES), F32)] * 2,
        compiler_params=_cparams(("arbitrary",)),
        name="merge",
    )(*args)


HALO = 8


def _expand_heads(fac, col0, is_a):
    parts = []
    for j in range(SSD_HPG // 2):
        a = fac[:, col0 + 2 * j:col0 + 2 * j + 1]
        b = fac[:, col0 + 2 * j + 1:col0 + 2 * j + 2]
        parts.append(jnp.where(is_a, a, b))
    return jnp.concatenate(parts, axis=1)


def _ssd_kernel(z_ref, xbc_ref, prev_ref, next_ref, dt_ref, cw_ref, cb_ref, dtb_ref, alog_ref,
                dskip_ref, nw_ref, out_ref, ext_ref, yf_ref, st_ref, *, n_chunks):
    q = SSD_CHUNK
    p = pl.program_id(1)
    c = pl.program_id(2)
    fwd = p == 0
    cc = jnp.where(fwd, c, n_chunks - 1 - c)

    @pl.when(c == 0)
    def _():
        st_ref[...] = jnp.zeros_like(st_ref)

    ext_ref[0:HALO, :] = jnp.where(cc == 0, 0.0, prev_ref[...])
    ext_ref[HALO:HALO + q, :] = xbc_ref[...]
    ext_ref[HALO + q:, :] = jnp.where(cc == n_chunks - 1, 0.0, next_ref[...])
    acc = jnp.broadcast_to(cb_ref[...], (q, CONV_CH))
    for k in range(SSD_CONV):
        acc = acc + cw_ref[k:k + 1, :] * ext_ref[pl.ds(HALO - SSD_CONV // 2 + k, q), :]
    xbc = _silu(acc)
    xs = xbc[:, :SSD_INNER]
    gn = SSD_GROUPS * SSD_STATE

    dt_all = _softplus(dt_ref[...] + dtb_ref[...])
    dta_all = dt_all * (-jnp.exp(alog_ref[...]))
    dt_d = jnp.where(fwd, dt_all, pltpu.roll(dt_all, LANES - SSD_HEADS, 1))
    dta_d = jnp.where(fwd, dta_all, pltpu.roll(dta_all, LANES - SSD_HEADS, 1))
    ri = lax.broadcasted_iota(jnp.int32, (q, q), 0)
    ci = lax.broadcasted_iota(jnp.int32, (q, q), 1)
    before = (ci - ri) * jnp.where(fwd, 1, -1) <= 0
    sums = jnp.dot(jnp.concatenate([before.astype(F32), jnp.ones((q, q), F32)], axis=0), dta_d,
                   precision=lax.Precision.HIGHEST, preferred_element_type=F32)
    cum = sums[:q]
    total = sums[q:]
    e_cum = jnp.exp(cum)
    d_state = jnp.exp(total - cum) * dt_d
    d_chunk = jnp.exp(total)
    cum_t = cum.T
    is_a = lax.broadcasted_iota(jnp.int32, (q, LANES), 1) < SSD_HEAD_DIM

    ys = []
    for g in range(SSD_GROUPS):
        col0 = g * SSD_HPG
        xg = xs[:, g * 512:(g + 1) * 512]
        bg = xbc[:, SSD_INNER + g * SSD_STATE:SSD_INNER + (g + 1) * SSD_STATE]
        cg = xbc[:, SSD_INNER + gn + g * SSD_STATE:SSD_INNER + gn + (g + 1) * SSD_STATE]
        bg16 = bg.astype(BF16)
        cg16 = cg.astype(BF16)
        cbm = lax.dot_general(cg16, bg16, (((1,), (1,)), ((), ())), preferred_element_type=F32)
        xdt = (xg * _expand_heads(dt_d, col0, is_a)).astype(BF16)
        y_parts = []
        for j in range(SSD_HPG // 2):
            ms = []
            for h in (2 * j, 2 * j + 1):
                col = col0 + h
                diff = cum[:, col:col + 1] - cum_t[col:col + 1, :]
                seg = jnp.where(before, jnp.exp(jnp.minimum(diff, 0.0)), 0.0)
                ms.append((cbm * seg).astype(BF16))
            y2 = jnp.dot(jnp.concatenate(ms, axis=0), xdt[:, j * LANES:(j + 1) * LANES],
                         preferred_element_type=F32)
            y_parts.append(jnp.where(is_a, y2[:q], y2[q:]))
        y_diag = jnp.concatenate(y_parts, axis=1)
        st = st_ref[g]
        y_off = jnp.dot(cg16, st.astype(BF16), preferred_element_type=F32) * _expand_heads(e_cum, col0, is_a)
        xd = (xg * _expand_heads(d_state, col0, is_a)).astype(BF16)
        new = jnp.dot(bg.T.astype(BF16), xd, preferred_element_type=F32)
        st_ref[g] = st * _expand_heads(d_chunk, col0, is_a) + new
        ys.append(y_diag + y_off)
    y = jnp.concatenate(ys, axis=1)
    row0 = pl.multiple_of(cc * q, q)

    @pl.when(fwd)
    def _():
        yf_ref[pl.ds(row0, q), :] = y

    @pl.when(jnp.logical_not(fwd))
    def _():
        yt = (yf_ref[pl.ds(row0, q), :] + y + xs * dskip_ref[...]) * _silu(z_ref[...])
        half = SSD_INNER // SSD_GROUPS
        for g in range(SSD_GROUPS):
            v = yt[:, g * half:(g + 1) * half]
            ms = jnp.mean(v * v, axis=-1, keepdims=True)
            out_ref[:, g * half:(g + 1) * half] = (
                v * lax.rsqrt(ms + RMS_EPS) * nw_ref[:, g * half:(g + 1) * half]).astype(BF16)


def _ssd(zux, dt, cw, cb, dtb, alog, dskip, nw, *, batch, seq):
    t = batch * seq
    q = SSD_CHUNK
    n_chunks = seq // q
    hb = q // HALO

    def chunk(p, c):
        return jnp.where(p == 0, c, n_chunks - 1 - c)

    def row_blk(b, p, c):
        return b * n_chunks + chunk(p, c)

    def prev_blk(b, p, c):
        return jnp.maximum(row_blk(b, p, c) * hb - 1, b * n_chunks * hb)

    def next_blk(b, p, c):
        return jnp.minimum((row_blk(b, p, c) + 1) * hb, (b + 1) * n_chunks * hb - 1)

    def z_blk(b, p, c):
        return jnp.where(p == 0, b * n_chunks + n_chunks - 1, row_blk(b, p, c))

    def const(shape):
        return pl.BlockSpec(shape, lambda b, p, c: (0, 0))

    return pl.pallas_call(
        functools.partial(_ssd_kernel, n_chunks=n_chunks),
        grid=(batch, 2, n_chunks),
        in_specs=[
            pl.BlockSpec((q, SSD_INNER), lambda b, p, c: (z_blk(b, p, c), 0)),
            pl.BlockSpec((q, CONV_CH), lambda b, p, c: (row_blk(b, p, c), 1)),
            pl.BlockSpec((HALO, CONV_CH), lambda b, p, c: (prev_blk(b, p, c), 1)),
            pl.BlockSpec((HALO, CONV_CH), lambda b, p, c: (next_blk(b, p, c), 1)),
            pl.BlockSpec((q, LANES), lambda b, p, c: (row_blk(b, p, c), 0)),
            const((8, CONV_CH)), const((1, CONV_CH)), const((1, LANES)), const((1, LANES)),
            const((1, SSD_INNER)), const((1, SSD_INNER)),
        ],
        out_specs=pl.BlockSpec((q, SSD_INNER), lambda b, p, c: (z_blk(b, p, c), 0)),
        out_shape=jax.ShapeDtypeStruct((t, SSD_INNER), BF16),
        scratch_shapes=[
            pltpu.VMEM((q + 2 * HALO, CONV_CH), F32),
            pltpu.VMEM((seq, SSD_INNER), F32),
            pltpu.VMEM((SSD_GROUPS, SSD_STATE, SSD_INNER // SSD_GROUPS), F32),
        ],
        compiler_params=_cparams(("arbitrary", "arbitrary", "arbitrary")),
        name="ssd",
    )(zux, zux, zux, zux, dt, cw, cb, dtb, alog, dskip, nw)


POOL_PAD = 16
POOL_ROWS = 512


def _pool_kernel(u_ref, w_ref, sc_ref, out_ref, pad_ref, *, seq):
    zeros = jnp.zeros((POOL_PAD, POOL_WIDTH), F32)
    pad_ref[0:POOL_PAD, :] = zeros
    pad_ref[POOL_PAD + seq:, :] = zeros
    pad_ref[POOL_PAD:POOL_PAD + seq, :] = u_ref[...]
    for r0 in range(0, seq, POOL_ROWS):
        pos = r0 + lax.broadcasted_iota(jnp.int32, (POOL_ROWS, POOL_GROUP), 0)
        for g, w in enumerate(POOL_WINDOWS):
            cols = slice(g * POOL_GROUP, (g + 1) * POOL_GROUP)
            u = pad_ref[POOL_PAD + r0:POOL_PAD + r0 + POOL_ROWS, cols]
            s = None
            for j in range(-(w // 2), w // 2):
                term = pad_ref[POOL_PAD + r0 + j:POOL_PAD + r0 + j + POOL_ROWS, cols]
                s = term if s is None else s + term
            cnt = jnp.minimum(pos + w // 2, seq) - jnp.maximum(pos - w // 2, 0)
            mean = s / cnt.astype(F32)
            y = jnp.dot((mean - u).astype(BF16), w_ref[g], preferred_element_type=F32)
            out_ref[r0:r0 + POOL_ROWS, cols] = (y * sc_ref[:, cols]).astype(BF16)


def _pool(zux, pw, psc, *, batch, seq):
    t = batch * seq
    return pl.pallas_call(
        functools.partial(_pool_kernel, seq=seq),
        scratch_shapes=[pltpu.VMEM((seq + 2 * POOL_PAD, POOL_WIDTH), F32)],
        grid=(batch,),
        in_specs=[
            pl.BlockSpec((seq, POOL_WIDTH), lambda b: (b, SSD_INNER // POOL_WIDTH)),
            pl.BlockSpec((len(POOL_WINDOWS), POOL_GROUP, POOL_GROUP), lambda b: (0, 0, 0)),
            pl.BlockSpec((1, POOL_WIDTH), lambda b: (0, 0)),
        ],
        out_specs=pl.BlockSpec((seq, POOL_WIDTH), lambda b: (b, 0)),
        out_shape=jax.ShapeDtypeStruct((t, POOL_WIDTH), BF16),
        compiler_params=_cparams(("arbitrary",)),
        name="pool",
    )(zux, pw, psc)


def _outproj_kernel(x_ref, a_ref, s_ref, p_ref, wa_ref, ws_ref, wp_ref, o_ref):
    acc = jnp.dot(a_ref[...], wa_ref[...], preferred_element_type=F32)
    acc = acc + jnp.dot(s_ref[...], ws_ref[...], preferred_element_type=F32)
    acc = acc + jnp.dot(p_ref[...], wp_ref[...], preferred_element_type=F32)
    o_ref[...] = x_ref[...] + acc


def _outproj(x2, att, ssd, pool, wa, ws, wp, *, tm, tn):
    t = x2.shape[0]
    return pl.pallas_call(
        _outproj_kernel,
        grid=(D_MODEL // tn, t // tm),
        in_specs=[
            pl.BlockSpec((tm, tn), lambda j, i: (i, j)),
            pl.BlockSpec((tm, ATT_OUT), lambda j, i: (i, 0)),
            pl.BlockSpec((tm, SSD_INNER), lambda j, i: (i, 0)),
            pl.BlockSpec((tm, POOL_WIDTH), lambda j, i: (i, 0)),
            pl.BlockSpec((ATT_OUT, tn), lambda j, i: (0, j)),
            pl.BlockSpec((SSD_INNER, tn), lambda j, i: (0, j)),
            pl.BlockSpec((POOL_WIDTH, tn), lambda j, i: (0, j)),
        ],
        out_specs=pl.BlockSpec((tm, tn), lambda j, i: (i, j)),
        out_shape=jax.ShapeDtypeStruct((t, D_MODEL), F32),
        compiler_params=_cparams(("arbitrary", "arbitrary")),
        name="outproj",
    )(x2, att, ssd, pool, wa, ws, wp)


def _ffn_kernel(x_ref, nw_ref, wg_ref, wu_ref, wd_ref, fw_ref, o_ref, h_ref, *, n_ff, final):
    j = pl.program_id(1)

    @pl.when(j == 0)
    def _():
        x = x_ref[...]
        ms = jnp.mean(x * x, axis=-1, keepdims=True)
        h_ref[...] = (x * lax.rsqrt(ms + RMS_EPS) * nw_ref[...]).astype(BF16)
        o_ref[...] = x

    h = h_ref[...]
    gate = jnp.dot(h, wg_ref[...], preferred_element_type=F32)
    up = jnp.dot(h, wu_ref[...], preferred_element_type=F32)
    act = (_silu(gate) * up).astype(BF16)
    o_ref[...] += jnp.dot(act, wd_ref[...], preferred_element_type=F32)

    if final:
        @pl.when(j == n_ff - 1)
        def _():
            y = o_ref[...]
            ms = jnp.mean(y * y, axis=-1, keepdims=True)
            o_ref[...] = y * lax.rsqrt(ms + RMS_EPS) * fw_ref[...]


def _ffn(x2, nw, wg, wu, wd, fw, *, tm, tf, final):
    t = x2.shape[0]
    n_ff = D_FF // tf
    return pl.pallas_call(
        functools.partial(_ffn_kernel, n_ff=n_ff, final=final),
        grid=(t // tm, n_ff),
        in_specs=[
            pl.BlockSpec((tm, D_MODEL), lambda i, j: (i, 0)),
            pl.BlockSpec((1, D_MODEL), lambda i, j: (0, 0)),
            pl.BlockSpec((D_MODEL, tf), lambda i, j: (0, j)),
            pl.BlockSpec((D_MODEL, tf), lambda i, j: (0, j)),
            pl.BlockSpec((tf, D_MODEL), lambda i, j: (j, 0)),
            pl.BlockSpec((1, D_MODEL), lambda i, j: (0, 0)),
        ],
        out_specs=pl.BlockSpec((tm, D_MODEL), lambda i, j: (i, 0)),
        out_shape=jax.ShapeDtypeStruct((t, D_MODEL), F32),
        scratch_shapes=[pltpu.VMEM((tm, D_MODEL), BF16)],
        compiler_params=_cparams(("arbitrary", "arbitrary")),
        name="ffn_final" if final else "ffn",
    )(x2, nw, wg, wu, wd, fw)


def _pack_in_weights(w_in):
    q0, k0, v0 = 0, ATT_WIDTH, 2 * ATT_WIDTH
    z0 = 3 * ATT_WIDTH
    xbc0 = z0 + SSD_INNER
    dt0 = xbc0 + CONV_CH
    u0 = dt0 + 2 * SSD_HEADS
    tiles = []
    for g in range(N_DIL):
        sl = slice(g * ATT_OUT, (g + 1) * ATT_OUT)
        tiles += [w_in[:, q0:k0][:, sl], w_in[:, k0:v0][:, sl], w_in[:, v0:z0][:, sl]]
    tiles += [w_in[:, z0:xbc0], w_in[:, u0:u0 + POOL_WIDTH], w_in[:, xbc0:dt0]]
    wcat = jnp.concatenate(tiles, axis=1).astype(BF16)
    wdt = jnp.pad(w_in[:, dt0:u0], ((0, 0), (0, LANES - 2 * SSD_HEADS))).astype(BF16)
    return wcat, wdt


def _pad_lanes(v):
    return jnp.pad(v.reshape(1, -1).astype(F32), ((0, 0), (0, LANES - v.size)))


def _layer(x2, slopes, p, fw, *, batch, seq, final):
    wcat, wdt = _pack_in_weights(p["w_in"])
    q0, q1, q2, zux, dt = _inproj(x2, p["norm1_w"].reshape(1, -1), wcat, wdt,
                                  batch=batch, seq=seq, tm=512)
    ol = [_attention_group(qkv, slopes[g], dil=d) for g, (d, qkv) in enumerate(zip(DILATIONS, (q0, q1, q2)))]
    att = _merge(ol, batch=batch, seq=seq, tm=512)
    cw = jnp.pad(p["conv_w"].astype(F32), ((0, 8 - SSD_CONV), (0, 0)))
    ssd = _ssd(zux, dt, cw, p["conv_b"].reshape(1, -1).astype(F32), _pad_lanes(p["dt_bias"]),
               _pad_lanes(p["a_log"]), jnp.repeat(p["d_skip"].astype(F32), SSD_HEAD_DIM).reshape(1, -1),
               p["ssd_norm_w"].reshape(1, -1).astype(F32), batch=batch, seq=seq)
    pool = _pool(zux, p["pool_w"].astype(BF16), p["pool_scale"].reshape(1, -1).astype(F32),
                 batch=batch, seq=seq)
    wo = p["w_out"].astype(BF16)
    x2 = _outproj(x2, att, ssd, pool, wo[:ATT_OUT], wo[ATT_OUT:ATT_OUT + SSD_INNER],
                  wo[ATT_OUT + SSD_INNER:], tm=1024, tn=1024)
    return _ffn(x2, p["norm2_w"].reshape(1, -1), p["w_gate"].astype(BF16), p["w_up"].astype(BF16),
                p["w_down"].astype(BF16), fw.reshape(1, -1), tm=512, tf=512, final=final)


def kernel(x, norm1_w, w_in, conv_w, conv_b, dt_bias, a_log, d_skip, ssd_norm_w, pool_w, pool_scale,
           w_out, norm2_w, w_gate, w_up, w_down, final_norm_w):
    batch, seq, _ = x.shape
    depth = w_in.shape[0]
    k = jnp.arange(1, ATT_HEADS + 1, dtype=F32)
    slopes = (2.0 ** (-8.0 * k / ATT_HEADS)).reshape(N_DIL, ATT_SLOTS)
    x2 = x.reshape(batch * seq, D_MODEL)
    for l in range(depth):
        p = dict(norm1_w=norm1_w[l], w_in=w_in[l], conv_w=conv_w[l], conv_b=conv_b[l], dt_bias=dt_bias[l],
                 a_log=a_log[l], d_skip=d_skip[l], ssd_norm_w=ssd_norm_w[l], pool_w=pool_w[l],
                 pool_scale=pool_scale[l], w_out=w_out[l], norm2_w=norm2_w[l], w_gate=w_gate[l],
                 w_up=w_up[l], w_down=w_down[l])
        x2 = _layer(x2, slopes, p, final_norm_w, batch=batch, seq=seq, final=(l == depth - 1))
    return x2.reshape(batch, seq, D_MODEL)
```

```python
import functools
import math

import jax
import jax.numpy as jnp
from jax import lax
from jax.experimental import pallas as pl
from jax.experimental.pallas import tpu as pltpu

F32 = jnp.float32
BF16 = jnp.bfloat16

D_MODEL = 2048
ATT_HEAD_DIM = 64
ATT_SLOTS = 8
DILATIONS = (1, 4, 16)
ATT_HALF = 64
N_DIL = len(DILATIONS)
ATT_HEADS = ATT_SLOTS * N_DIL
ATT_WIDTH = ATT_HEADS * ATT_HEAD_DIM
ATT_OUT = ATT_SLOTS * ATT_HEAD_DIM
SSD_HEAD_DIM = 64
SSD_INNER = D_MODEL // 2
SSD_HEADS = SSD_INNER // SSD_HEAD_DIM
SSD_GROUPS = 2
SSD_HPG = SSD_HEADS // SSD_GROUPS
SSD_STATE = 128
SSD_CONV = 5
SSD_CHUNK = 128
CONV_CH = SSD_INNER + 2 * SSD_GROUPS * SSD_STATE
POOL_WINDOWS = (2, 4, 8, 16)
POOL_GROUP = 128
POOL_WIDTH = POOL_GROUP * len(POOL_WINDOWS)
MIX_WIDTH = ATT_OUT + SSD_INNER + POOL_WIDTH
D_FF = 5632
RMS_EPS = 1e-6
NEG_INF = -1e30

LANES = 128
IN_TILE = 1536
N_IN_TILES = 5
VMEM_LIMIT = 56 * 1024 * 1024
OUT_TN = 1024
FFN_TM = 1024
FFN_TF = 512


def _cparams(sem):
    return pltpu.CompilerParams(dimension_semantics=sem, vmem_limit_bytes=VMEM_LIMIT)


def _silu(x):
    return x * (1.0 / (1.0 + jnp.exp(-x)))


def _softplus(x):
    return jnp.maximum(x, 0.0) + jnp.log(1.0 + jnp.exp(-jnp.abs(x)))


def _inproj_kernel(x_ref, nw_ref, w_ref, wdt_ref, q0_ref, q1_ref, q2_ref, zux_ref, dt_ref,
                   h_ref, stage_ref, *, tm):
    j = pl.program_id(1)

    @pl.when(j == 0)
    def _():
        x = x_ref[...]
        ms = jnp.mean(x * x, axis=-1, keepdims=True)
        h = x * lax.rsqrt(ms + RMS_EPS) * nw_ref[...]
        h_ref[...] = h.astype(BF16)
        dt_ref[...] = jnp.dot(h_ref[...], wdt_ref[...], preferred_element_type=F32)

    def proj():
        return jnp.dot(h_ref[...], w_ref[0], preferred_element_type=F32)

    @pl.when(j == 0)
    def _():
        q0_ref[0, 0] = proj().astype(BF16)

    for g, (d, out_ref) in enumerate(zip(DILATIONS, (q0_ref, q1_ref, q2_ref))):
        if g == 0:
            continue

        @pl.when(j == g)
        def _(d=d, out_ref=out_ref):
            res = proj()
            for cb in range(IN_TILE // LANES):
                stage_ref[cb] = res[:, cb * LANES:(cb + 1) * LANES]
            for r in range(d):
                for cb in range(IN_TILE // LANES):
                    out_ref[0, r, :, cb * LANES:(cb + 1) * LANES] = (
                        stage_ref[cb, pl.ds(r, tm // d, stride=d), :].astype(BF16))

    @pl.when(j >= N_DIL)
    def _():
        zux_ref[...] = proj()


def _inproj(x2, nw, wcat, wdt, *, batch, seq, tm):
    t = x2.shape[0]
    per_b = seq // tm
    grid = (t // tm, N_IN_TILES)

    def qspec(d):
        return pl.BlockSpec((1, d, tm // d, IN_TILE), lambda i, j: (i // per_b, 0, i % per_b, 0))

    out_shape = [jax.ShapeDtypeStruct((batch, d, seq // d, IN_TILE), BF16) for d in DILATIONS]
    out_shape += [jax.ShapeDtypeStruct((t, 2 * IN_TILE), F32), jax.ShapeDtypeStruct((t, LANES), F32)]
    return pl.pallas_call(
        functools.partial(_inproj_kernel, tm=tm),
        grid=grid,
        in_specs=[
            pl.BlockSpec((tm, D_MODEL), lambda i, j: (i, 0)),
            pl.BlockSpec((1, D_MODEL), lambda i, j: (0, 0)),
            pl.BlockSpec((1, D_MODEL, IN_TILE), lambda i, j: (j, 0, 0)),
            pl.BlockSpec((D_MODEL, LANES), lambda i, j: (0, 0)),
        ],
        out_specs=[qspec(d) for d in DILATIONS] + [
            pl.BlockSpec((tm, IN_TILE), lambda i, j: (i, jnp.maximum(j - N_DIL, 0))),
            pl.BlockSpec((tm, LANES), lambda i, j: (i, 0)),
        ],
        out_shape=out_shape,
        scratch_shapes=[pltpu.VMEM((tm, D_MODEL), BF16), pltpu.VMEM((IN_TILE // LANES, tm, LANES), F32)],
        compiler_params=_cparams(("arbitrary", "arbitrary")),
        name="inproj",
    )(x2, nw, wcat, wdt)


ATT_QB = 128
ATT_KW = 256


ATT_UNROLL = 8


def _attn_kernel(slopes_ref, q_ref, k_ref, v_ref, o_ref, lse_ref, bias_ref, *, seq_len, dil):
    sp = pl.program_id(0)
    is_a = lax.broadcasted_iota(jnp.int32, (ATT_QB, LANES), 1) < ATT_HEAD_DIM
    n_blocks = seq_len // ATT_QB

    @pl.when(pl.program_id(1) == 0)
    def _():
        slope_a = slopes_ref[2 * sp] * float(dil)
        slope_b = slopes_ref[2 * sp + 1] * float(dil)
        rows = lax.broadcasted_iota(jnp.int32, (2 * ATT_QB, ATT_KW), 0)
        cols = lax.broadcasted_iota(jnp.int32, (2 * ATT_QB, ATT_KW), 1)
        rel_base = cols - (rows & (ATT_QB - 1))
        slope = jnp.where(rows < ATT_QB, slope_a, slope_b)
        for t in range(3):
            dist = jnp.abs(rel_base - t * ATT_HALF)
            bias_ref[t] = jnp.where(dist <= ATT_HALF, -slope * dist.astype(F32), NEG_INF)

    def body(it, carry):
        r = lax.shift_right_logical(it, n_blocks.bit_length() - 1)
        i = it & (n_blocks - 1)
        q0 = pl.multiple_of(i * ATT_QB, ATT_QB)
        k0 = pl.multiple_of(jnp.clip(i * ATT_QB - ATT_HALF, 0, seq_len - ATT_KW), ATT_HALF)
        q2 = q_ref[0, r, pl.ds(q0, ATT_QB), :] * jnp.asarray(ATT_HEAD_DIM ** -0.5, BF16)
        zero = jnp.zeros_like(q2)
        qs = jnp.concatenate([jnp.where(is_a, q2, zero), jnp.where(is_a, zero, q2)], axis=0)
        kk = k_ref[0, r, pl.ds(k0, ATT_KW), :]
        vv = v_ref[0, r, pl.ds(k0, ATT_KW), :]
        s = lax.dot_general(qs, kk, (((1,), (1,)), ((), ())), preferred_element_type=F32)
        s = s + bias_ref[lax.shift_right_logical(q0 - k0, ATT_HALF.bit_length() - 1)]
        m = jnp.max(s, axis=-1, keepdims=True)
        p = jnp.exp(s - m)
        l = jnp.sum(p, axis=-1, keepdims=True)
        pv = jnp.dot(p.astype(BF16), vv, preferred_element_type=F32)
        o = pv * (1.0 / l)
        lse = jnp.broadcast_to(m + jnp.log(l), (2 * ATT_QB, LANES))
        o_ref[0, r, pl.ds(q0, ATT_QB), :] = jnp.where(is_a, o[:ATT_QB], o[ATT_QB:])
        lse_ref[0, r, pl.ds(q0, ATT_QB), :] = jnp.where(is_a, lse[:ATT_QB], lse[ATT_QB:])
        return carry

    lax.fori_loop(0, dil * n_blocks, body, 0, unroll=ATT_UNROLL)


def _attention_group(qkv, slopes, *, dil):
    batch, d, seq_len, _ = qkv.shape
    n_pairs = ATT_OUT // LANES

    def spec(col0):
        return pl.BlockSpec((1, d, seq_len, LANES), lambda sp, b: (b, 0, 0, col0 + sp))

    out = jax.ShapeDtypeStruct((batch, d, seq_len, ATT_OUT), F32)
    return pl.pallas_call(
        functools.partial(_attn_kernel, seq_len=seq_len, dil=dil),
        grid=(n_pairs, batch),
        in_specs=[pl.BlockSpec(memory_space=pltpu.SMEM), spec(0), spec(n_pairs), spec(2 * n_pairs)],
        out_specs=[spec(0), spec(0)],
        out_shape=[out, out],
        scratch_shapes=[pltpu.VMEM((3, 2 * ATT_QB, ATT_KW), F32)],
        compiler_params=_cparams(("arbitrary", "arbitrary")),
        name=f"attn_d{dil}",
    )(slopes, qkv, qkv, qkv)


def _merge_kernel(o0_ref, l0_ref, o1_ref, l1_ref, o2_ref, l2_ref, att_ref, so_ref, sl_ref, *, tm):
    outs = [o0_ref[0, 0]]
    lses = [l0_ref[0, 0]]
    for d, o_ref, l_ref in ((DILATIONS[1], o1_ref, l1_ref), (DILATIONS[2], o2_ref, l2_ref)):
        for r in range(d):
            for cb in range(ATT_OUT // LANES):
                cols = slice(cb * LANES, (cb + 1) * LANES)
                so_ref[cb, pl.ds(r, tm // d, stride=d), :] = o_ref[0, r, :, cols]
                sl_ref[cb, pl.ds(r, tm // d, stride=d), :] = l_ref[0, r, :, cols]
        outs.append(jnp.concatenate([so_ref[cb] for cb in range(ATT_OUT // LANES)], axis=1))
        lses.append(jnp.concatenate([sl_ref[cb] for cb in range(ATT_OUT // LANES)], axis=1))
    mx = jnp.maximum(jnp.maximum(lses[0], lses[1]), lses[2])
    ws = [jnp.exp(l - mx) for l in lses]
    num = ws[0] * outs[0] + ws[1] * outs[1] + ws[2] * outs[2]
    att_ref[...] = (num / (ws[0] + ws[1] + ws[2])).astype(BF16)


def _merge(ol, *, batch, seq, tm):
    t = batch * seq
    per_b = seq // tm
    in_specs = []
    args = []
    for d, (o, l) in zip(DILATIONS, ol):
        spec = pl.BlockSpec((1, d, tm // d, ATT_OUT), lambda i: (i // per_b, 0, i % per_b, 0))
        in_specs += [spec, spec]
        args += [o, l]
    return pl.pallas_call(
        functools.partial(_merge_kernel, tm=tm),
        grid=(t // tm,),
        in_specs=in_specs,
        out_specs=pl.BlockSpec((tm, ATT_OUT), lambda i: (i, 0)),
        out_shape=jax.ShapeDtypeStruct((t, ATT_OUT), BF16),
        scratch_shapes=[pltpu.VMEM((ATT_OUT // LANES, tm, LANES), F32)] * 2,
        compiler_params=_cparams(("arbitrary",)),
        name="merge",
    )(*args)


HALO = 8


def _expand_heads(fac, col0, is_a):
    parts = []
    for j in range(SSD_HPG // 2):
        a = fac[:, col0 + 2 * j:col0 + 2 * j + 1]
        b = fac[:, col0 + 2 * j + 1:col0 + 2 * j + 2]
        parts.append(jnp.where(is_a, a, b))
    return jnp.concatenate(parts, axis=1)


def _ssd_kernel(z_ref, xbc_ref, prev_ref, next_ref, dt_ref, cw_ref, cb_ref, dtb_ref, alog_ref,
                dskip_ref, nw_ref, out_ref, ext_ref, xc_ref, yf_ref, st_ref, *, n_chunks):
    q = SSD_CHUNK
    p = pl.program_id(1)
    c = pl.program_id(2)
    fwd = p == 0
    cc = jnp.where(fwd, c, n_chunks - 1 - c)

    @pl.when(c == 0)
    def _():
        st_ref[...] = jnp.zeros_like(st_ref)

    row0 = pl.multiple_of(cc * q, q)

    @pl.when(fwd)
    def _():
        ext_ref[0:HALO, :] = jnp.where(cc == 0, 0.0, prev_ref[...])
        ext_ref[HALO:HALO + q, :] = xbc_ref[...]
        ext_ref[HALO + q:, :] = jnp.where(cc == n_chunks - 1, 0.0, next_ref[...])
        acc = jnp.broadcast_to(cb_ref[...], (q, CONV_CH))
        for k in range(SSD_CONV):
            acc = acc + cw_ref[k:k + 1, :] * ext_ref[pl.ds(HALO - SSD_CONV // 2 + k, q), :]
        xc_ref[pl.ds(row0, q), :] = _silu(acc)

    xs = xc_ref[pl.ds(row0, q), 0:SSD_INNER]
    gn = SSD_GROUPS * SSD_STATE

    dt_all = _softplus(dt_ref[...] + dtb_ref[...])
    dta_all = dt_all * (-jnp.exp(alog_ref[...]))
    dt_d = jnp.where(fwd, dt_all, pltpu.roll(dt_all, LANES - SSD_HEADS, 1))
    dta_d = jnp.where(fwd, dta_all, pltpu.roll(dta_all, LANES - SSD_HEADS, 1))
    ri = lax.broadcasted_iota(jnp.int32, (q, q), 0)
    ci = lax.broadcasted_iota(jnp.int32, (q, q), 1)
    before = (ci - ri) * jnp.where(fwd, 1, -1) <= 0
    sums = jnp.dot(jnp.concatenate([before.astype(F32), jnp.ones((q, q), F32)], axis=0), dta_d,
                   precision=lax.Precision.HIGHEST, preferred_element_type=F32)
    cum = sums[:q]
    total = sums[q:]
    e_cum = jnp.exp(cum)
    d_state = jnp.exp(total - cum) * dt_d
    d_chunk = jnp.exp(total[0:8])
    log_dt_t = jnp.log(dt_d).T
    shifted_t = cum.T - log_dt_t
    is_a = lax.broadcasted_iota(jnp.int32, (q, LANES), 1) < SSD_HEAD_DIM

    ys = []
    for g in range(SSD_GROUPS):
        col0 = g * SSD_HPG
        xg = xs[:, g * 512:(g + 1) * 512]
        bg = xc_ref[pl.ds(row0, q), SSD_INNER + g * SSD_STATE:SSD_INNER + (g + 1) * SSD_STATE]
        cg = xc_ref[pl.ds(row0, q), SSD_INNER + gn + g * SSD_STATE:SSD_INNER + gn + (g + 1) * SSD_STATE]
        xg16 = xg.astype(BF16)
        bg16 = bg.astype(BF16)
        cg16 = cg.astype(BF16)
        cbm = lax.dot_general(cg16, bg16, (((1,), (1,)), ((), ())), preferred_element_type=F32)
        cbm = jnp.where(before, cbm, 0.0)
        y_parts = []
        for j in range(SSD_HPG // 2):
            ms = []
            for h in (2 * j, 2 * j + 1):
                col = col0 + h
                diff = cum[:, col:col + 1] - shifted_t[col:col + 1, :]
                ms.append((cbm * jnp.exp(jnp.minimum(diff, log_dt_t[col:col + 1, :]))).astype(BF16))
            y2 = jnp.dot(jnp.concatenate(ms, axis=0), xg16[:, j * LANES:(j + 1) * LANES],
                         preferred_element_type=F32)
            y_parts.append(jnp.where(is_a, y2[:q], y2[q:]))
        y_diag = jnp.concatenate(y_parts, axis=1)
        st = st_ref[g]
        y_off = jnp.dot(cg16, st.astype(BF16), preferred_element_type=F32) * _expand_heads(e_cum, col0, is_a)
        xd = (xg * _expand_heads(d_state, col0, is_a)).astype(BF16)
        new = jnp.dot(bg.T.astype(BF16), xd, preferred_element_type=F32)
        keep = _expand_heads(d_chunk, col0, is_a[0:8])
        st_ref[g] = (st.reshape(SSD_STATE // 8, 8, -1) * keep[None]).reshape(st.shape) + new
        ys.append(y_diag + y_off)
    y = jnp.concatenate(ys, axis=1)

    @pl.when(fwd)
    def _():
        yf_ref[pl.ds(row0, q), :] = y

    @pl.when(jnp.logical_not(fwd))
    def _():
        yt = (yf_ref[pl.ds(row0, q), :] + y + xs * dskip_ref[...]) * _silu(z_ref[...])
        half = SSD_INNER // SSD_GROUPS
        for g in range(SSD_GROUPS):
            v = yt[:, g * half:(g + 1) * half]
            ms = jnp.mean(v * v, axis=-1, keepdims=True)
            out_ref[:, g * half:(g + 1) * half] = (
                v * lax.rsqrt(ms + RMS_EPS) * nw_ref[:, g * half:(g + 1) * half]).astype(BF16)


def _ssd(zux, dt, cw, cb, dtb, alog, dskip, nw, *, batch, seq):
    t = batch * seq
    q = SSD_CHUNK
    n_chunks = seq // q
    hb = q // HALO

    def chunk(p, c):
        return jnp.where(p == 0, c, n_chunks - 1 - c)

    def row_blk(b, p, c):
        return b * n_chunks + chunk(p, c)

    def conv_blk(b, p, c):
        return b * n_chunks + jnp.where(p == 0, c, n_chunks - 1)

    def prev_blk(b, p, c):
        return jnp.maximum(conv_blk(b, p, c) * hb - 1, b * n_chunks * hb)

    def next_blk(b, p, c):
        return jnp.minimum((conv_blk(b, p, c) + 1) * hb, (b + 1) * n_chunks * hb - 1)

    def z_blk(b, p, c):
        return jnp.where(p == 0, b * n_chunks + n_chunks - 1, row_blk(b, p, c))

    def const(shape):
        return pl.BlockSpec(shape, lambda b, p, c: (0, 0))

    return pl.pallas_call(
        functools.partial(_ssd_kernel, n_chunks=n_chunks),
        grid=(batch, 2, n_chunks),
        in_specs=[
            pl.BlockSpec((q, SSD_INNER), lambda b, p, c: (z_blk(b, p, c), 0)),
            pl.BlockSpec((q, CONV_CH), lambda b, p, c: (conv_blk(b, p, c), 1)),
            pl.BlockSpec((HALO, CONV_CH), lambda b, p, c: (prev_blk(b, p, c), 1)),
            pl.BlockSpec((HALO, CONV_CH), lambda b, p, c: (next_blk(b, p, c), 1)),
            pl.BlockSpec((q, LANES), lambda b, p, c: (row_blk(b, p, c), 0)),
            const((8, CONV_CH)), const((1, CONV_CH)), const((1, LANES)), const((1, LANES)),
            const((1, SSD_INNER)), const((1, SSD_INNER)),
        ],
        out_specs=pl.BlockSpec((q, SSD_INNER), lambda b, p, c: (z_blk(b, p, c), 0)),
        out_shape=jax.ShapeDtypeStruct((t, SSD_INNER), BF16),
        scratch_shapes=[
            pltpu.VMEM((q + 2 * HALO, CONV_CH), F32),
            pltpu.VMEM((seq, CONV_CH), F32),
            pltpu.VMEM((seq, SSD_INNER), F32),
            pltpu.VMEM((SSD_GROUPS, SSD_STATE, SSD_INNER // SSD_GROUPS), F32),
        ],
        compiler_params=_cparams(("arbitrary", "arbitrary", "arbitrary")),
        name="ssd",
    )(zux, zux, zux, zux, dt, cw, cb, dtb, alog, dskip, nw)


POOL_PAD = 16
POOL_ROWS = 512


def _pool_kernel(u_ref, w_ref, sc_ref, out_ref, pad_ref, *, seq):
    zeros = jnp.zeros((POOL_PAD, POOL_WIDTH), F32)
    pad_ref[0:POOL_PAD, :] = zeros
    pad_ref[POOL_PAD + seq:, :] = zeros
    pad_ref[POOL_PAD:POOL_PAD + seq, :] = u_ref[...]
    for r0 in range(0, seq, POOL_ROWS):
        pos = r0 + lax.broadcasted_iota(jnp.int32, (POOL_ROWS, POOL_GROUP), 0)
        for g, w in enumerate(POOL_WINDOWS):
            cols = slice(g * POOL_GROUP, (g + 1) * POOL_GROUP)
            u = pad_ref[POOL_PAD + r0:POOL_PAD + r0 + POOL_ROWS, cols]
            s = None
            for j in range(-(w // 2), w // 2):
                term = pad_ref[POOL_PAD + r0 + j:POOL_PAD + r0 + j + POOL_ROWS, cols]
                s = term if s is None else s + term
            cnt = jnp.minimum(pos + w // 2, seq) - jnp.maximum(pos - w // 2, 0)
            mean = s / cnt.astype(F32)
            y = jnp.dot((mean - u).astype(BF16), w_ref[g], preferred_element_type=F32)
            out_ref[r0:r0 + POOL_ROWS, cols] = (y * sc_ref[:, cols]).astype(BF16)


def _pool(zux, pw, psc, *, batch, seq):
    t = batch * seq
    return pl.pallas_call(
        functools.partial(_pool_kernel, seq=seq),
        scratch_shapes=[pltpu.VMEM((seq + 2 * POOL_PAD, POOL_WIDTH), F32)],
        grid=(batch,),
        in_specs=[
            pl.BlockSpec((seq, POOL_WIDTH), lambda b: (b, SSD_INNER // POOL_WIDTH)),
            pl.BlockSpec((len(POOL_WINDOWS), POOL_GROUP, POOL_GROUP), lambda b: (0, 0, 0)),
            pl.BlockSpec((1, POOL_WIDTH), lambda b: (0, 0)),
        ],
        out_specs=pl.BlockSpec((seq, POOL_WIDTH), lambda b: (b, 0)),
        out_shape=jax.ShapeDtypeStruct((t, POOL_WIDTH), BF16),
        compiler_params=_cparams(("arbitrary",)),
        name="pool",
    )(zux, pw, psc)


def _outproj_kernel(x_ref, a_ref, s_ref, p_ref, wa_ref, ws_ref, wp_ref, o_ref):
    acc = jnp.dot(a_ref[...], wa_ref[0], preferred_element_type=F32)
    acc = acc + jnp.dot(s_ref[...], ws_ref[0], preferred_element_type=F32)
    acc = acc + jnp.dot(p_ref[...], wp_ref[0], preferred_element_type=F32)
    o_ref[...] = x_ref[...] + acc


def _outproj(x2, att, ssd, pool, wa, ws, wp, *, tm, tn):
    t = x2.shape[0]
    return pl.pallas_call(
        _outproj_kernel,
        grid=(D_MODEL // tn, t // tm),
        in_specs=[
            pl.BlockSpec((tm, tn), lambda j, i: (i, j)),
            pl.BlockSpec((tm, ATT_OUT), lambda j, i: (i, 0)),
            pl.BlockSpec((tm, SSD_INNER), lambda j, i: (i, 0)),
            pl.BlockSpec((tm, POOL_WIDTH), lambda j, i: (i, 0)),
            pl.BlockSpec((1, ATT_OUT, tn), lambda j, i: (j, 0, 0)),
            pl.BlockSpec((1, SSD_INNER, tn), lambda j, i: (j, 0, 0)),
            pl.BlockSpec((1, POOL_WIDTH, tn), lambda j, i: (j, 0, 0)),
        ],
        out_specs=pl.BlockSpec((tm, tn), lambda j, i: (i, j)),
        out_shape=jax.ShapeDtypeStruct((t, D_MODEL), F32),
        compiler_params=_cparams(("arbitrary", "arbitrary")),
        name="outproj",
    )(x2, att, ssd, pool, wa, ws, wp)


def _ffn_kernel(x_ref, nw_ref, wg_ref, wu_ref, wd_ref, fw_ref, o_ref, h_ref, *, n_ff, final):
    j = pl.program_id(1)

    @pl.when(j == 0)
    def _():
        x = x_ref[...]
        ms = jnp.mean(x * x, axis=-1, keepdims=True)
        h_ref[...] = (x * lax.rsqrt(ms + RMS_EPS) * nw_ref[...]).astype(BF16)
        o_ref[...] = x

    h = h_ref[...]
    gate = jnp.dot(h, wg_ref[0], preferred_element_type=F32)
    up = jnp.dot(h, wu_ref[0], preferred_element_type=F32)
    act = (_silu(gate) * up).astype(BF16)
    o_ref[...] += jnp.dot(act, wd_ref[...], preferred_element_type=F32)

    if final:
        @pl.when(j == n_ff - 1)
        def _():
            y = o_ref[...]
            ms = jnp.mean(y * y, axis=-1, keepdims=True)
            o_ref[...] = y * lax.rsqrt(ms + RMS_EPS) * fw_ref[...]


def _ffn(x2, nw, wg, wu, wd, fw, *, tm, tf, final):
    t = x2.shape[0]
    n_ff = D_FF // tf
    return pl.pallas_call(
        functools.partial(_ffn_kernel, n_ff=n_ff, final=final),
        grid=(t // tm, n_ff),
        in_specs=[
            pl.BlockSpec((tm, D_MODEL), lambda i, j: (i, 0)),
            pl.BlockSpec((1, D_MODEL), lambda i, j: (0, 0)),
            pl.BlockSpec((1, D_MODEL, tf), lambda i, j: (j, 0, 0)),
            pl.BlockSpec((1, D_MODEL, tf), lambda i, j: (j, 0, 0)),
            pl.BlockSpec((tf, D_MODEL), lambda i, j: (j, 0)),
            pl.BlockSpec((1, D_MODEL), lambda i, j: (0, 0)),
        ],
        out_specs=pl.BlockSpec((tm, D_MODEL), lambda i, j: (i, 0)),
        out_shape=jax.ShapeDtypeStruct((t, D_MODEL), F32),
        scratch_shapes=[pltpu.VMEM((tm, D_MODEL), BF16)],
        compiler_params=_cparams(("arbitrary", "arbitrary")),
        name="ffn_final" if final else "ffn",
    )(x2, nw, wg, wu, wd, fw)


def _col_tiles(w, tn):
    k, n = w.shape
    return w.reshape(k, n // tn, tn).transpose(1, 0, 2)


def _pack_in_weights(w_in):
    q0, k0, v0 = 0, ATT_WIDTH, 2 * ATT_WIDTH
    z0 = 3 * ATT_WIDTH
    xbc0 = z0 + SSD_INNER
    dt0 = xbc0 + CONV_CH
    u0 = dt0 + 2 * SSD_HEADS
    tiles = []
    for g in range(N_DIL):
        sl = slice(g * ATT_OUT, (g + 1) * ATT_OUT)
        tiles += [w_in[:, q0:k0][:, sl], w_in[:, k0:v0][:, sl], w_in[:, v0:z0][:, sl]]
    tiles += [w_in[:, z0:xbc0], w_in[:, u0:u0 + POOL_WIDTH], w_in[:, xbc0:dt0]]
    wcat = jnp.concatenate(tiles, axis=1).astype(BF16)
    wcat = _col_tiles(wcat, IN_TILE)
    wdt = jnp.pad(w_in[:, dt0:u0], ((0, 0), (0, LANES - 2 * SSD_HEADS))).astype(BF16)
    return wcat, wdt


def _pad_lanes(v):
    return jnp.pad(v.reshape(1, -1).astype(F32), ((0, 0), (0, LANES - v.size)))


def _layer(x2, slopes, p, fw, *, batch, seq, final):
    wcat, wdt = _pack_in_weights(p["w_in"])
    q0, q1, q2, zux, dt = _inproj(x2, p["norm1_w"].reshape(1, -1), wcat, wdt,
                                  batch=batch, seq=seq, tm=512)
    ol = [_attention_group(qkv, slopes[g], dil=d) for g, (d, qkv) in enumerate(zip(DILATIONS, (q0, q1, q2)))]
    att = _merge(ol, batch=batch, seq=seq, tm=512)
    cw = jnp.pad(p["conv_w"].astype(F32), ((0, 8 - SSD_CONV), (0, 0)))
    ssd = _ssd(zux, dt, cw, p["conv_b"].reshape(1, -1).astype(F32), _pad_lanes(p["dt_bias"]),
               _pad_lanes(p["a_log"]), jnp.repeat(p["d_skip"].astype(F32), SSD_HEAD_DIM).reshape(1, -1),
               p["ssd_norm_w"].reshape(1, -1).astype(F32), batch=batch, seq=seq)
    pool = _pool(zux, p["pool_w"].astype(BF16), p["pool_scale"].reshape(1, -1).astype(F32),
                 batch=batch, seq=seq)
    wo = _col_tiles(p["w_out"].astype(BF16), OUT_TN)
    x2 = _outproj(x2, att, ssd, pool, wo[:, :ATT_OUT], wo[:, ATT_OUT:ATT_OUT + SSD_INNER],
                  wo[:, ATT_OUT + SSD_INNER:], tm=1024, tn=OUT_TN)
    return _ffn(x2, p["norm2_w"].reshape(1, -1), _col_tiles(p["w_gate"].astype(BF16), FFN_TF),
                _col_tiles(p["w_up"].astype(BF16), FFN_TF), p["w_down"].astype(BF16), fw.reshape(1, -1),
                tm=FFN_TM, tf=FFN_TF, final=final)


def kernel(x, norm1_w, w_in, conv_w, conv_b, dt_bias, a_log, d_skip, ssd_norm_w, pool_w, pool_scale,
           w_out, norm2_w, w_gate, w_up, w_down, final_norm_w):
    batch, seq, _ = x.shape
    depth = w_in.shape[0]
    k = jnp.arange(1, ATT_HEADS + 1, dtype=F32)
    slopes = (2.0 ** (-8.0 * k / ATT_HEADS)).reshape(N_DIL, ATT_SLOTS)
    x2 = x.reshape(batch * seq, D_MODEL)
    for l in range(depth):
        p = dict(norm1_w=norm1_w[l], w_in=w_in[l], conv_w=conv_w[l], conv_b=conv_b[l], dt_bias=dt_bias[l],
                 a_log=a_log[l], d_skip=d_skip[l], ssd_norm_w=ssd_norm_w[l], pool_w=pool_w[l],
                 pool_scale=pool_scale[l], w_out=w_out[l], norm2_w=norm2_w[l], w_gate=w_gate[l],
                 w_up=w_up[l], w_down=w_down[l])
        x2 = _layer(x2, slopes, p, final_norm_w, batch=batch, seq=seq, final=(l == depth - 1))
    return x2.reshape(batch, seq, D_MODEL)
```

```python
import functools
import math

import jax
import jax.numpy as jnp
from jax import lax
from jax.experimental import pallas as pl
from jax.experimental.pallas import tpu as pltpu

F32 = jnp.float32
BF16 = jnp.bfloat16

D_MODEL = 2048
ATT_HEAD_DIM = 64
ATT_SLOTS = 8
DILATIONS = (1, 4, 16)
ATT_HALF = 64
N_DIL = len(DILATIONS)
ATT_HEADS = ATT_SLOTS * N_DIL
ATT_WIDTH = ATT_HEADS * ATT_HEAD_DIM
ATT_OUT = ATT_SLOTS * ATT_HEAD_DIM
SSD_HEAD_DIM = 64
SSD_INNER = D_MODEL // 2
SSD_HEADS = SSD_INNER // SSD_HEAD_DIM
SSD_GROUPS = 2
SSD_HPG = SSD_HEADS // SSD_GROUPS
SSD_STATE = 128
SSD_CONV = 5
SSD_CHUNK = 128
CONV_CH = SSD_INNER + 2 * SSD_GROUPS * SSD_STATE
POOL_WINDOWS = (2, 4, 8, 16)
POOL_GROUP = 128
POOL_WIDTH = POOL_GROUP * len(POOL_WINDOWS)
MIX_WIDTH = ATT_OUT + SSD_INNER + POOL_WIDTH
D_FF = 5632
RMS_EPS = 1e-6
NEG_INF = -1e30

LANES = 128
IN_TILE = 768
IN_TILES_PER_GROUP = 2
N_IN_TILES = 5 * IN_TILES_PER_GROUP
IN_TM = 1024
VMEM_LIMIT = 56 * 1024 * 1024
OUT_TN = 1024
FFN_TM = 1024
FFN_TF = 512


def _cparams(sem):
    return pltpu.CompilerParams(dimension_semantics=sem, vmem_limit_bytes=VMEM_LIMIT)


def _silu(x):
    return x * (1.0 / (1.0 + jnp.exp(-x)))


def _softplus(x):
    return jnp.maximum(x, 0.0) + jnp.log(1.0 + jnp.exp(-jnp.abs(x)))


def _inproj_kernel(x_ref, nw_ref, w_ref, wdt_ref, q0_ref, q1_ref, q2_ref, zux_ref, dt_ref,
                   h_ref, stage_ref, *, tm):
    j = pl.program_id(1)
    jg = j // IN_TILES_PER_GROUP

    @pl.when(j == 0)
    def _():
        x = x_ref[...]
        ms = jnp.mean(x * x, axis=-1, keepdims=True)
        h = x * lax.rsqrt(ms + RMS_EPS) * nw_ref[...]
        h_ref[...] = h.astype(BF16)
        dt_ref[...] = jnp.dot(h_ref[...], wdt_ref[...], preferred_element_type=F32)

    def proj():
        return jnp.dot(h_ref[...], w_ref[...], preferred_element_type=F32)

    @pl.when(jg == 0)
    def _():
        q0_ref[0, 0] = proj().astype(BF16)

    for g, (d, out_ref) in enumerate(zip(DILATIONS, (q0_ref, q1_ref, q2_ref))):
        if g == 0:
            continue

        @pl.when(jg == g)
        def _(d=d, out_ref=out_ref):
            res = proj()
            for cb in range(IN_TILE // LANES):
                stage_ref[cb] = res[:, cb * LANES:(cb + 1) * LANES]
            for r in range(d):
                for cb in range(IN_TILE // LANES):
                    out_ref[0, r, :, cb * LANES:(cb + 1) * LANES] = (
                        stage_ref[cb, pl.ds(r, tm // d, stride=d), :].astype(BF16))

    @pl.when(jg >= N_DIL)
    def _():
        zux_ref[...] = proj()


def _inproj(x2, nw, wcat, wdt, *, batch, seq, tm):
    t = x2.shape[0]
    per_b = seq // tm
    grid = (t // tm, N_IN_TILES)
    tpg = IN_TILES_PER_GROUP

    def qspec(g, d):
        return pl.BlockSpec((1, d, tm // d, IN_TILE),
                            lambda i, j: (i // per_b, 0, i % per_b, jnp.clip(j - g * tpg, 0, tpg - 1)))

    out_shape = [jax.ShapeDtypeStruct((batch, d, seq // d, tpg * IN_TILE), BF16) for d in DILATIONS]
    out_shape += [jax.ShapeDtypeStruct((t, 2 * tpg * IN_TILE), F32), jax.ShapeDtypeStruct((t, LANES), F32)]
    return pl.pallas_call(
        functools.partial(_inproj_kernel, tm=tm),
        grid=grid,
        in_specs=[
            pl.BlockSpec((tm, D_MODEL), lambda i, j: (i, 0)),
            pl.BlockSpec((1, D_MODEL), lambda i, j: (0, 0)),
            pl.BlockSpec((D_MODEL, IN_TILE), lambda i, j: (0, j)),
            pl.BlockSpec((D_MODEL, LANES), lambda i, j: (0, 0)),
        ],
        out_specs=[qspec(g, d) for g, d in enumerate(DILATIONS)] + [
            pl.BlockSpec((tm, IN_TILE), lambda i, j: (i, jnp.maximum(j - N_DIL * tpg, 0))),
            pl.BlockSpec((tm, LANES), lambda i, j: (i, 0)),
        ],
        out_shape=out_shape,
        scratch_shapes=[pltpu.VMEM((tm, D_MODEL), BF16), pltpu.VMEM((IN_TILE // LANES, tm, LANES), F32)],
        compiler_params=_cparams(("arbitrary", "arbitrary")),
        name="inproj",
    )(x2, nw, wcat, wdt)


ATT_QB = 128
ATT_KW = 256
ATT_UNROLL = 8


def _attn_kernel(slopes_ref, q_ref, k_ref, v_ref, o_ref, lse_ref, bias_ref, *, seq_len, dil):
    sp = pl.program_id(0)
    is_a = lax.broadcasted_iota(jnp.int32, (ATT_QB, LANES), 1) < ATT_HEAD_DIM
    n_blocks = seq_len // ATT_QB

    @pl.when(pl.program_id(1) == 0)
    def _():
        slope_a = slopes_ref[2 * sp] * float(dil)
        slope_b = slopes_ref[2 * sp + 1] * float(dil)
        rows = lax.broadcasted_iota(jnp.int32, (2 * ATT_QB, ATT_KW), 0)
        cols = lax.broadcasted_iota(jnp.int32, (2 * ATT_QB, ATT_KW), 1)
        rel_base = cols - (rows & (ATT_QB - 1))
        slope = jnp.where(rows < ATT_QB, slope_a, slope_b)
        for t in range(3):
            dist = jnp.abs(rel_base - t * ATT_HALF)
            bias_ref[t] = jnp.where(dist <= ATT_HALF, -slope * dist.astype(F32), NEG_INF)

    def body(it, carry):
        r = lax.shift_right_logical(it, n_blocks.bit_length() - 1)
        i = it & (n_blocks - 1)
        q0 = pl.multiple_of(i * ATT_QB, ATT_QB)
        k0 = pl.multiple_of(jnp.clip(i * ATT_QB - ATT_HALF, 0, seq_len - ATT_KW), ATT_HALF)
        q2 = q_ref[0, r, pl.ds(q0, ATT_QB), :] * jnp.asarray(ATT_HEAD_DIM ** -0.5, BF16)
        zero = jnp.zeros_like(q2)
        qs = jnp.concatenate([jnp.where(is_a, q2, zero), jnp.where(is_a, zero, q2)], axis=0)
        kk = k_ref[0, r, pl.ds(k0, ATT_KW), :]
        vv = v_ref[0, r, pl.ds(k0, ATT_KW), :]
        s = lax.dot_general(qs, kk, (((1,), (1,)), ((), ())), preferred_element_type=F32)
        s = s + bias_ref[lax.shift_right_logical(q0 - k0, ATT_HALF.bit_length() - 1)]
        m = jnp.max(s, axis=-1, keepdims=True)
        p = jnp.exp(s - m)
        l = jnp.sum(p, axis=-1, keepdims=True)
        pv = jnp.dot(p.astype(BF16), vv, preferred_element_type=F32)
        o = pv * (1.0 / l)
        lse = jnp.broadcast_to(m + jnp.log(l), (2 * ATT_QB, LANES))
        o_ref[0, r, pl.ds(q0, ATT_QB), :] = jnp.where(is_a, o[:ATT_QB], o[ATT_QB:])
        lse_ref[0, r, pl.ds(q0, ATT_QB), :] = jnp.where(is_a, lse[:ATT_QB], lse[ATT_QB:])
        return carry

    lax.fori_loop(0, dil * n_blocks, body, 0, unroll=ATT_UNROLL)


def _attention_group(qkv, slopes, *, dil):
    batch, d, seq_len, _ = qkv.shape
    n_pairs = ATT_OUT // LANES

    def spec(col0):
        return pl.BlockSpec((1, d, seq_len, LANES), lambda sp, b: (b, 0, 0, col0 + sp))

    out = jax.ShapeDtypeStruct((batch, d, seq_len, ATT_OUT), F32)
    return pl.pallas_call(
        functools.partial(_attn_kernel, seq_len=seq_len, dil=dil),
        grid=(n_pairs, batch),
        in_specs=[pl.BlockSpec(memory_space=pltpu.SMEM), spec(0), spec(n_pairs), spec(2 * n_pairs)],
        out_specs=[spec(0), spec(0)],
        out_shape=[out, out],
        scratch_shapes=[pltpu.VMEM((3, 2 * ATT_QB, ATT_KW), F32)],
        compiler_params=_cparams(("arbitrary", "arbitrary")),
        name=f"attn_d{dil}",
    )(slopes, qkv, qkv, qkv)


def _merge_kernel(o0_ref, l0_ref, o1_ref, l1_ref, o2_ref, l2_ref, att_ref, so_ref, sl_ref, *, tm):
    outs = [o0_ref[0, 0]]
    lses = [l0_ref[0, 0]]
    for d, o_ref, l_ref in ((DILATIONS[1], o1_ref, l1_ref), (DILATIONS[2], o2_ref, l2_ref)):
        for r in range(d):
            for cb in range(ATT_OUT // LANES):
                cols = slice(cb * LANES, (cb + 1) * LANES)
                so_ref[cb, pl.ds(r, tm // d, stride=d), :] = o_ref[0, r, :, cols]
                sl_ref[cb, pl.ds(r, tm // d, stride=d), :] = l_ref[0, r, :, cols]
        outs.append(jnp.concatenate([so_ref[cb] for cb in range(ATT_OUT // LANES)], axis=1))
        lses.append(jnp.concatenate([sl_ref[cb] for cb in range(ATT_OUT // LANES)], axis=1))
    mx = jnp.maximum(jnp.maximum(lses[0], lses[1]), lses[2])
    ws = [jnp.exp(l - mx) for l in lses]
    num = ws[0] * outs[0] + ws[1] * outs[1] + ws[2] * outs[2]
    att_ref[...] = (num / (ws[0] + ws[1] + ws[2])).astype(BF16)


def _merge(ol, *, batch, seq, tm):
    t = batch * seq
    per_b = seq // tm
    in_specs = []
    args = []
    for d, (o, l) in zip(DILATIONS, ol):
        spec = pl.BlockSpec((1, d, tm // d, ATT_OUT), lambda i: (i // per_b, 0, i % per_b, 0))
        in_specs += [spec, spec]
        args += [o, l]
    return pl.pallas_call(
        functools.partial(_merge_kernel, tm=tm),
        grid=(t // tm,),
        in_specs=in_specs,
        out_specs=pl.BlockSpec((tm, ATT_OUT), lambda i: (i, 0)),
        out_shape=jax.ShapeDtypeStruct((t, ATT_OUT), BF16),
        scratch_shapes=[pltpu.VMEM((ATT_OUT // LANES, tm, LANES), F32)] * 2,
        compiler_params=_cparams(("arbitrary",)),
        name="merge",
    )(*args)


HALO = 8
SSD_SUB = 2
SSD_ROWS = SSD_SUB * SSD_CHUNK


def _expand_rows(fac, col0, is_a):
    parts = []
    for j in range(SSD_HPG // 2):
        a = fac[:, col0 + 2 * j:col0 + 2 * j + 1]
        b = fac[:, col0 + 2 * j + 1:col0 + 2 * j + 2]
        parts.append(jnp.where(is_a, a, b))
    return jnp.concatenate(parts, axis=1)


def _split2(v):
    hi = v.astype(BF16)
    lo = (v - hi.astype(F32)).astype(BF16)
    return jnp.concatenate([hi, lo], axis=1)


def _ssd_kernel(z_ref, xbc_ref, prev_ref, next_ref, dt_ref, cw_ref, cb_ref, dtb_ref, alog_ref,
                dskip_ref, nw_ref, out_ref, ext_ref, xc_ref, yf_ref, st_ref, sel_ref, *, n_blocks):
    q = SSD_CHUNK
    p = pl.program_id(1)
    c = pl.program_id(2)
    fwd = p == 0
    cc = jnp.where(fwd, c, n_blocks - 1 - c)
    row0 = pl.multiple_of(cc * SSD_ROWS, SSD_ROWS)

    @pl.when(jnp.logical_and(fwd, c == 0))
    def _():
        yf_ref[...] = jnp.zeros_like(yf_ref)

    @pl.when(c == 0)
    def _():
        st_ref[...] = jnp.zeros_like(st_ref)
        src = lax.broadcasted_iota(jnp.int32, (2 * LANES, SSD_INNER), 0) & (LANES - 1)
        dst = lax.shift_right_logical(lax.broadcasted_iota(jnp.int32, (2 * LANES, SSD_INNER), 1),
                                      SSD_HEAD_DIM.bit_length() - 1)
        sel_ref[...] = jnp.where(src == dst, 1.0, 0.0).astype(BF16)

    @pl.when(fwd)
    def _():
        ext_ref[0:HALO, :] = jnp.where(cc == 0, 0.0, prev_ref[...])
        ext_ref[HALO:HALO + SSD_ROWS, :] = xbc_ref[...]
        ext_ref[HALO + SSD_ROWS:, :] = jnp.where(cc == n_blocks - 1, 0.0, next_ref[...])
        acc = jnp.broadcast_to(cb_ref[...], (SSD_ROWS, CONV_CH))
        for k in range(SSD_CONV):
            acc = acc + cw_ref[k:k + 1, :] * ext_ref[pl.ds(HALO - SSD_CONV // 2 + k, SSD_ROWS), :]
        xc_ref[pl.ds(row0, SSD_ROWS), :] = _silu(acc)

    gn = SSD_GROUPS * SSD_STATE
    half = SSD_INNER // SSD_GROUPS
    ri = lax.broadcasted_iota(jnp.int32, (q, q), 0)
    ci = lax.broadcasted_iota(jnp.int32, (q, q), 1)
    before = (ci - ri) * jnp.where(fwd, 1, -1) <= 0
    cum_lhs = jnp.concatenate([before.astype(F32), jnp.ones((q, q), F32)], axis=0)
    is_a = lax.broadcasted_iota(jnp.int32, (q, LANES), 1) < SSD_HEAD_DIM
    neg_a = -jnp.exp(alog_ref[...])

    for k in range(SSD_SUB):
        off = pl.multiple_of(jnp.where(fwd, k, SSD_SUB - 1 - k) * q, q)
        rows = pl.ds(row0 + off, q)
        xs = xc_ref[rows, 0:SSD_INNER]

        dt_all = _softplus(dt_ref[pl.ds(off, q), :] + dtb_ref[...])
        dta_all = dt_all * neg_a
        dt_d = jnp.where(fwd, dt_all, pltpu.roll(dt_all, LANES - SSD_HEADS, 1))
        dta_d = jnp.where(fwd, dta_all, pltpu.roll(dta_all, LANES - SSD_HEADS, 1))
        sums = jnp.dot(cum_lhs, dta_d, precision=lax.Precision.HIGHEST, preferred_element_type=F32)
        cum = sums[:q]
        total = sums[q:]
        wide = jnp.dot(jnp.concatenate([_split2(jnp.exp(cum)), _split2(jnp.exp(total - cum) * dt_d)], axis=0),
                       sel_ref[...], preferred_element_type=F32)
        e_cum_w = wide[:q]
        d_state_w = wide[q:]
        d_chunk = jnp.exp(total[0:8])
        log_dt_t = jnp.log(dt_d).T
        shifted_t = cum.T - log_dt_t

        ys = []
        for g in range(SSD_GROUPS):
            col0 = g * SSD_HPG
            gcols = slice(g * half, (g + 1) * half)
            xg = xs[:, gcols]
            bg = xc_ref[rows, SSD_INNER + g * SSD_STATE:SSD_INNER + (g + 1) * SSD_STATE]
            cg = xc_ref[rows, SSD_INNER + gn + g * SSD_STATE:SSD_INNER + gn + (g + 1) * SSD_STATE]
            xg16 = xg.astype(BF16)
            bg16 = bg.astype(BF16)
            cg16 = cg.astype(BF16)
            cbm = lax.dot_general(cg16, bg16, (((1,), (1,)), ((), ())), preferred_element_type=F32)
            cbm = jnp.where(before, cbm, 0.0)
            y_parts = []
            for j in range(SSD_HPG // 2):
                ms = []
                for h in (2 * j, 2 * j + 1):
                    col = col0 + h
                    diff = cum[:, col:col + 1] - shifted_t[col:col + 1, :]
                    ms.append((cbm * jnp.exp(jnp.minimum(diff, log_dt_t[col:col + 1, :]))).astype(BF16))
                y2 = jnp.dot(jnp.concatenate(ms, axis=0), xg16[:, j * LANES:(j + 1) * LANES],
                             preferred_element_type=F32)
                y_parts.append(jnp.where(is_a, y2[:q], y2[q:]))
            y_diag = jnp.concatenate(y_parts, axis=1)
            st = st_ref[g]
            y_off = jnp.dot(cg16, st.astype(BF16), preferred_element_type=F32) * e_cum_w[:, gcols]
            xd = (xg * d_state_w[:, gcols]).astype(BF16)
            new = jnp.dot(bg.T.astype(BF16), xd, preferred_element_type=F32)
            keep = _expand_rows(d_chunk, col0, is_a[0:8])
            st_ref[g] = (st.reshape(SSD_STATE // 8, 8, -1) * keep[None]).reshape(st.shape) + new
            ys.append(y_diag + y_off)
        y = jnp.concatenate(ys, axis=1)
        yf_ref[rows, :] += y

    @pl.when(jnp.logical_not(fwd))
    def _():
        rows = pl.ds(row0, SSD_ROWS)
        yt = (yf_ref[rows, :] + xc_ref[rows, 0:SSD_INNER] * dskip_ref[...]) * _silu(z_ref[...])
        for g in range(SSD_GROUPS):
            v = yt[:, g * half:(g + 1) * half]
            ms = jnp.mean(v * v, axis=-1, keepdims=True)
            out_ref[:, g * half:(g + 1) * half] = (
                v * lax.rsqrt(ms + RMS_EPS) * nw_ref[:, g * half:(g + 1) * half]).astype(BF16)


def _ssd(zux, dt, cw, cb, dtb, alog, dskip, nw, *, batch, seq):
    t = batch * seq
    rows = SSD_ROWS
    n_blocks = seq // rows
    hb = rows // HALO

    def row_blk(b, p, c):
        return b * n_blocks + jnp.where(p == 0, c, n_blocks - 1 - c)

    def conv_blk(b, p, c):
        return b * n_blocks + jnp.where(p == 0, c, n_blocks - 1)

    def prev_blk(b, p, c):
        return jnp.maximum(conv_blk(b, p, c) * hb - 1, b * n_blocks * hb)

    def next_blk(b, p, c):
        return jnp.minimum((conv_blk(b, p, c) + 1) * hb, (b + 1) * n_blocks * hb - 1)

    def z_blk(b, p, c):
        return jnp.where(p == 0, b * n_blocks + n_blocks - 1, row_blk(b, p, c))

    def const(shape):
        return pl.BlockSpec(shape, lambda b, p, c: (0, 0))

    return pl.pallas_call(
        functools.partial(_ssd_kernel, n_blocks=n_blocks),
        grid=(batch, 2, n_blocks),
        in_specs=[
            pl.BlockSpec((rows, SSD_INNER), lambda b, p, c: (z_blk(b, p, c), 0)),
            pl.BlockSpec((rows, CONV_CH), lambda b, p, c: (conv_blk(b, p, c), 1)),
            pl.BlockSpec((HALO, CONV_CH), lambda b, p, c: (prev_blk(b, p, c), 1)),
            pl.BlockSpec((HALO, CONV_CH), lambda b, p, c: (next_blk(b, p, c), 1)),
            pl.BlockSpec((rows, LANES), lambda b, p, c: (row_blk(b, p, c), 0)),
            const((8, CONV_CH)), const((1, CONV_CH)), const((1, LANES)), const((1, LANES)),
            const((1, SSD_INNER)), const((1, SSD_INNER)),
        ],
        out_specs=pl.BlockSpec((rows, SSD_INNER), lambda b, p, c: (z_blk(b, p, c), 0)),
        out_shape=jax.ShapeDtypeStruct((t, SSD_INNER), BF16),
        scratch_shapes=[
            pltpu.VMEM((rows + 2 * HALO, CONV_CH), F32),
            pltpu.VMEM((seq, CONV_CH), F32),
            pltpu.VMEM((seq, SSD_INNER), F32),
            pltpu.VMEM((SSD_GROUPS, SSD_STATE, SSD_INNER // SSD_GROUPS), F32),
            pltpu.VMEM((2 * LANES, SSD_INNER), BF16),
        ],
        compiler_params=_cparams(("arbitrary", "arbitrary", "arbitrary")),
        name="ssd",
    )(zux, zux, zux, zux, dt, cw, cb, dtb, alog, dskip, nw)


POOL_PAD = 16
POOL_ROWS = 512


def _pool_kernel(u_ref, w_ref, sc_ref, out_ref, pad_ref, *, seq):
    zeros = jnp.zeros((POOL_PAD, POOL_WIDTH), F32)
    pad_ref[0:POOL_PAD, :] = zeros
    pad_ref[POOL_PAD + seq:, :] = zeros
    pad_ref[POOL_PAD:POOL_PAD + seq, :] = u_ref[...]
    for r0 in range(0, seq, POOL_ROWS):
        pos = r0 + lax.broadcasted_iota(jnp.int32, (POOL_ROWS, POOL_GROUP), 0)
        for g, w in enumerate(POOL_WINDOWS):
            cols = slice(g * POOL_GROUP, (g + 1) * POOL_GROUP)
            u = pad_ref[POOL_PAD + r0:POOL_PAD + r0 + POOL_ROWS, cols]
            s = None
            for j in range(-(w // 2), w // 2):
                term = pad_ref[POOL_PAD + r0 + j:POOL_PAD + r0 + j + POOL_ROWS, cols]
                s = term if s is None else s + term
            cnt = jnp.minimum(pos + w // 2, seq) - jnp.maximum(pos - w // 2, 0)
            mean = s / cnt.astype(F32)
            y = jnp.dot((mean - u).astype(BF16), w_ref[g], preferred_element_type=F32)
            out_ref[r0:r0 + POOL_ROWS, cols] = (y * sc_ref[:, cols]).astype(BF16)


def _pool(zux, pw, psc, *, batch, seq):
    t = batch * seq
    return pl.pallas_call(
        functools.partial(_pool_kernel, seq=seq),
        scratch_shapes=[pltpu.VMEM((seq + 2 * POOL_PAD, POOL_WIDTH), F32)],
        grid=(batch,),
        in_specs=[
            pl.BlockSpec((seq, POOL_WIDTH), lambda b: (b, SSD_INNER // POOL_WIDTH)),
            pl.BlockSpec((len(POOL_WINDOWS), POOL_GROUP, POOL_GROUP), lambda b: (0, 0, 0)),
            pl.BlockSpec((1, POOL_WIDTH), lambda b: (0, 0)),
        ],
        out_specs=pl.BlockSpec((seq, POOL_WIDTH), lambda b: (b, 0)),
        out_shape=jax.ShapeDtypeStruct((t, POOL_WIDTH), BF16),
        compiler_params=_cparams(("arbitrary",)),
        name="pool",
    )(zux, pw, psc)


def _outproj_kernel(x_ref, a_ref, s_ref, p_ref, wa_ref, ws_ref, wp_ref, o_ref):
    acc = jnp.dot(a_ref[...], wa_ref[...], preferred_element_type=F32)
    acc = acc + jnp.dot(s_ref[...], ws_ref[...], preferred_element_type=F32)
    acc = acc + jnp.dot(p_ref[...], wp_ref[...], preferred_element_type=F32)
    o_ref[...] = x_ref[...] + acc


def _outproj(x2, att, ssd, pool, wa, ws, wp, *, tm, tn):
    t = x2.shape[0]
    return pl.pallas_call(
        _outproj_kernel,
        grid=(D_MODEL // tn, t // tm),
        in_specs=[
            pl.BlockSpec((tm, tn), lambda j, i: (i, j)),
            pl.BlockSpec((tm, ATT_OUT), lambda j, i: (i, 0)),
            pl.BlockSpec((tm, SSD_INNER), lambda j, i: (i, 0)),
            pl.BlockSpec((tm, POOL_WIDTH), lambda j, i: (i, 0)),
            pl.BlockSpec((ATT_OUT, tn), lambda j, i: (0, j)),
            pl.BlockSpec((SSD_INNER, tn), lambda j, i: (0, j)),
            pl.BlockSpec((POOL_WIDTH, tn), lambda j, i: (0, j)),
        ],
        out_specs=pl.BlockSpec((tm, tn), lambda j, i: (i, j)),
        out_shape=jax.ShapeDtypeStruct((t, D_MODEL), F32),
        compiler_params=_cparams(("arbitrary", "arbitrary")),
        name="outproj",
    )(x2, att, ssd, pool, wa, ws, wp)


def _ffn_kernel(x_ref, nw_ref, wg_ref, wu_ref, wd_ref, fw_ref, o_ref, h_ref, *, n_ff, final):
    j = pl.program_id(1)

    @pl.when(j == 0)
    def _():
        x = x_ref[...]
        ms = jnp.mean(x * x, axis=-1, keepdims=True)
        h_ref[...] = (x * lax.rsqrt(ms + RMS_EPS) * nw_ref[...]).astype(BF16)
        o_ref[...] = x

    h = h_ref[...]
    gate = jnp.dot(h, wg_ref[...], preferred_element_type=F32)
    up = jnp.dot(h, wu_ref[...], preferred_element_type=F32)
    act = (_silu(gate) * up).astype(BF16)
    o_ref[...] += jnp.dot(act, wd_ref[...], preferred_element_type=F32)

    if final:
        @pl.when(j == n_ff - 1)
        def _():
            y = o_ref[...]
            ms = jnp.mean(y * y, axis=-1, keepdims=True)
            o_ref[...] = y * lax.rsqrt(ms + RMS_EPS) * fw_ref[...]


def _ffn(x2, nw, wg, wu, wd, fw, *, tm, tf, final):
    t = x2.shape[0]
    n_ff = D_FF // tf
    return pl.pallas_call(
        functools.partial(_ffn_kernel, n_ff=n_ff, final=final),
        grid=(t // tm, n_ff),
        in_specs=[
            pl.BlockSpec((tm, D_MODEL), lambda i, j: (i, 0)),
            pl.BlockSpec((1, D_MODEL), lambda i, j: (0, 0)),
            pl.BlockSpec((D_MODEL, tf), lambda i, j: (0, j)),
            pl.BlockSpec((D_MODEL, tf), lambda i, j: (0, j)),
            pl.BlockSpec((tf, D_MODEL), lambda i, j: (j, 0)),
            pl.BlockSpec((1, D_MODEL), lambda i, j: (0, 0)),
        ],
        out_specs=pl.BlockSpec((tm, D_MODEL), lambda i, j: (i, 0)),
        out_shape=jax.ShapeDtypeStruct((t, D_MODEL), F32),
        scratch_shapes=[pltpu.VMEM((tm, D_MODEL), BF16)],
        compiler_params=_cparams(("arbitrary", "arbitrary")),
        name="ffn_final" if final else "ffn",
    )(x2, nw, wg, wu, wd, fw)


def _pack_in_weights(w_in):
    q0, k0, v0 = 0, ATT_WIDTH, 2 * ATT_WIDTH
    z0 = 3 * ATT_WIDTH
    xbc0 = z0 + SSD_INNER
    dt0 = xbc0 + CONV_CH
    u0 = dt0 + 2 * SSD_HEADS
    tiles = []
    for g in range(N_DIL):
        sl = slice(g * ATT_OUT, (g + 1) * ATT_OUT)
        tiles += [w_in[:, q0:k0][:, sl], w_in[:, k0:v0][:, sl], w_in[:, v0:z0][:, sl]]
    tiles += [w_in[:, z0:xbc0], w_in[:, u0:u0 + POOL_WIDTH], w_in[:, xbc0:dt0]]
    wcat = jnp.concatenate(tiles, axis=1).astype(BF16)
    wdt = jnp.pad(w_in[:, dt0:u0], ((0, 0), (0, LANES - 2 * SSD_HEADS))).astype(BF16)
    return wcat, wdt


def _pad_lanes(v):
    return jnp.pad(v.reshape(1, -1).astype(F32), ((0, 0), (0, LANES - v.size)))


def _layer(x2, slopes, p, fw, *, batch, seq, final):
    wcat, wdt = _pack_in_weights(p["w_in"])
    q0, q1, q2, zux, dt = _inproj(x2, p["norm1_w"].reshape(1, -1), wcat, wdt,
                                  batch=batch, seq=seq, tm=IN_TM)
    ol = [_attention_group(qkv, slopes[g], dil=d) for g, (d, qkv) in enumerate(zip(DILATIONS, (q0, q1, q2)))]
    att = _merge(ol, batch=batch, seq=seq, tm=512)
    cw = jnp.pad(p["conv_w"].astype(F32), ((0, 8 - SSD_CONV), (0, 0)))
    ssd = _ssd(zux, dt, cw, p["conv_b"].reshape(1, -1).astype(F32), _pad_lanes(p["dt_bias"]),
               _pad_lanes(p["a_log"]), jnp.repeat(p["d_skip"].astype(F32), SSD_HEAD_DIM).reshape(1, -1),
               p["ssd_norm_w"].reshape(1, -1).astype(F32), batch=batch, seq=seq)
    pool = _pool(zux, p["pool_w"].astype(BF16), p["pool_scale"].reshape(1, -1).astype(F32),
                 batch=batch, seq=seq)
    wo = p["w_out"].astype(BF16)
    x2 = _outproj(x2, att, ssd, pool, wo[:ATT_OUT], wo[ATT_OUT:ATT_OUT + SSD_INNER],
                  wo[ATT_OUT + SSD_INNER:], tm=1024, tn=OUT_TN)
    return _ffn(x2, p["norm2_w"].reshape(1, -1), p["w_gate"].astype(BF16), p["w_up"].astype(BF16),
                p["w_down"].astype(BF16), fw.reshape(1, -1), tm=FFN_TM, tf=FFN_TF, final=final)


def kernel(x, norm1_w, w_in, conv_w, conv_b, dt_bias, a_log, d_skip, ssd_norm_w, pool_w, pool_scale,
           w_out, norm2_w, w_gate, w_up, w_down, final_norm_w):
    batch, seq, _ = x.shape
    depth = w_in.shape[0]
    k = jnp.arange(1, ATT_HEADS + 1, dtype=F32)
    slopes = (2.0 ** (-8.0 * k / ATT_HEADS)).reshape(N_DIL, ATT_SLOTS)
    x2 = x.reshape(batch * seq, D_MODEL)
    for l in range(depth):
        p = dict(norm1_w=norm1_w[l], w_in=w_in[l], conv_w=conv_w[l], conv_b=conv_b[l], dt_bias=dt_bias[l],
                 a_log=a_log[l], d_skip=d_skip[l], ssd_norm_w=ssd_norm_w[l], pool_w=pool_w[l],
                 pool_scale=pool_scale[l], w_out=w_out[l], norm2_w=norm2_w[l], w_gate=w_gate[l],
                 w_up=w_up[l], w_down=w_down[l])
        x2 = _layer(x2, slopes, p, final_norm_w, batch=batch, seq=seq, final=(l == depth - 1))
    return x2.reshape(batch, seq, D_MODEL)
```

```python
import functools
import math

import jax
import jax.numpy as jnp
from jax import lax
from jax.experimental import pallas as pl
from jax.experimental.pallas import tpu as pltpu

F32 = jnp.float32
BF16 = jnp.bfloat16

D_MODEL = 2048
ATT_HEAD_DIM = 64
ATT_SLOTS = 8
DILATIONS = (1, 4, 16)
ATT_HALF = 64
N_DIL = len(DILATIONS)
ATT_HEADS = ATT_SLOTS * N_DIL
ATT_WIDTH = ATT_HEADS * ATT_HEAD_DIM
ATT_OUT = ATT_SLOTS * ATT_HEAD_DIM
SSD_HEAD_DIM = 64
SSD_INNER = D_MODEL // 2
SSD_HEADS = SSD_INNER // SSD_HEAD_DIM
SSD_GROUPS = 2
SSD_HPG = SSD_HEADS // SSD_GROUPS
SSD_STATE = 128
SSD_CONV = 5
SSD_CHUNK = 128
CONV_CH = SSD_INNER + 2 * SSD_GROUPS * SSD_STATE
POOL_WINDOWS = (2, 4, 8, 16)
POOL_GROUP = 128
POOL_WIDTH = POOL_GROUP * len(POOL_WINDOWS)
MIX_WIDTH = ATT_OUT + SSD_INNER + POOL_WIDTH
D_FF = 5632
RMS_EPS = 1e-6
NEG_INF = -1e30

LANES = 128
IN_TILE = 768
IN_TILES_PER_GROUP = 2
N_IN_TILES = 5 * IN_TILES_PER_GROUP
IN_TM = 1024
VMEM_LIMIT = 56 * 1024 * 1024
OUT_TN = 1024
FFN_TM = 1024
FFN_TF = 512


def _cparams(sem):
    return pltpu.CompilerParams(dimension_semantics=sem, vmem_limit_bytes=VMEM_LIMIT)


def _silu(x):
    return x * (1.0 / (1.0 + jnp.exp(-x)))


def _softplus(x):
    return jnp.maximum(x, 0.0) + jnp.log(1.0 + jnp.exp(-jnp.abs(x)))


def _inproj_kernel(x_ref, nw_ref, wq_ref, wz_ref, wdt_ref, q0_ref, q1_ref, q2_ref, zux_ref, dt_ref,
                   h_ref, stage_ref, *, tm):
    j = pl.program_id(1)
    jg = j // IN_TILES_PER_GROUP

    @pl.when(j == 0)
    def _():
        x = x_ref[...]
        ms = jnp.mean(x * x, axis=-1, keepdims=True)
        h = x * lax.rsqrt(ms + RMS_EPS) * nw_ref[...]
        h_ref[...] = h.astype(BF16)
        dt_ref[...] = jnp.dot(h_ref[...], wdt_ref[0], preferred_element_type=F32)

    def proj(w_ref):
        return jnp.dot(h_ref[...], w_ref[0], preferred_element_type=F32)

    @pl.when(jg == 0)
    def _():
        q0_ref[0, 0] = proj(wq_ref).astype(BF16)

    for g, (d, out_ref) in enumerate(zip(DILATIONS, (q0_ref, q1_ref, q2_ref))):
        if g == 0:
            continue

        @pl.when(jg == g)
        def _(d=d, out_ref=out_ref):
            res = proj(wq_ref)
            for cb in range(IN_TILE // LANES):
                stage_ref[cb] = res[:, cb * LANES:(cb + 1) * LANES]
            for r in range(d):
                for cb in range(IN_TILE // LANES):
                    out_ref[0, r, :, cb * LANES:(cb + 1) * LANES] = (
                        stage_ref[cb, pl.ds(r, tm // d, stride=d), :].astype(BF16))

    @pl.when(jg >= N_DIL)
    def _():
        zux_ref[...] = proj(wz_ref)


def _inproj(x2, nw, wqkv, wzux, wdt, layer, *, batch, seq, tm):
    t = x2.shape[0]
    per_b = seq // tm
    grid = (t // tm, N_IN_TILES)
    tpg = IN_TILES_PER_GROUP
    n_q = N_DIL * tpg

    def qspec(g, d):
        return pl.BlockSpec((1, d, tm // d, IN_TILE),
                            lambda i, j: (i // per_b, 0, i % per_b, jnp.clip(j - g * tpg, 0, tpg - 1)))

    out_shape = [jax.ShapeDtypeStruct((batch, d, seq // d, tpg * IN_TILE), BF16) for d in DILATIONS]
    out_shape += [jax.ShapeDtypeStruct((t, 2 * tpg * IN_TILE), F32), jax.ShapeDtypeStruct((t, LANES), F32)]
    return pl.pallas_call(
        functools.partial(_inproj_kernel, tm=tm),
        grid=grid,
        in_specs=[
            pl.BlockSpec((tm, D_MODEL), lambda i, j: (i, 0)),
            pl.BlockSpec((1, D_MODEL), lambda i, j: (0, 0)),
            pl.BlockSpec((1, D_MODEL, IN_TILE), lambda i, j: (layer, 0, jnp.minimum(j, n_q - 1))),
            pl.BlockSpec((1, D_MODEL, IN_TILE), lambda i, j: (layer, 0, jnp.maximum(j - n_q, 0))),
            pl.BlockSpec((1, D_MODEL, LANES), lambda i, j: (layer, 0, 0)),
        ],
        out_specs=[qspec(g, d) for g, d in enumerate(DILATIONS)] + [
            pl.BlockSpec((tm, IN_TILE), lambda i, j: (i, jnp.maximum(j - n_q, 0))),
            pl.BlockSpec((tm, LANES), lambda i, j: (i, 0)),
        ],
        out_shape=out_shape,
        scratch_shapes=[pltpu.VMEM((tm, D_MODEL), BF16), pltpu.VMEM((IN_TILE // LANES, tm, LANES), F32)],
        compiler_params=_cparams(("arbitrary", "arbitrary")),
        name="inproj",
    )(x2, nw, wqkv, wzux, wdt)


ATT_QB = 128
ATT_KW = 256
ATT_UNROLL = 8


def _attn_kernel(slopes_ref, q_ref, k_ref, v_ref, o_ref, lse_ref, bias_ref, *, seq_len, dil):
    sp = pl.program_id(0)
    is_a = lax.broadcasted_iota(jnp.int32, (ATT_QB, LANES), 1) < ATT_HEAD_DIM
    n_blocks = seq_len // ATT_QB

    @pl.when(pl.program_id(1) == 0)
    def _():
        slope_a = slopes_ref[2 * sp] * float(dil)
        slope_b = slopes_ref[2 * sp + 1] * float(dil)
        rows = lax.broadcasted_iota(jnp.int32, (2 * ATT_QB, ATT_KW), 0)
        cols = lax.broadcasted_iota(jnp.int32, (2 * ATT_QB, ATT_KW), 1)
        rel_base = cols - (rows & (ATT_QB - 1))
        slope = jnp.where(rows < ATT_QB, slope_a, slope_b)
        for t in range(3):
            dist = jnp.abs(rel_base - t * ATT_HALF)
            bias_ref[t] = jnp.where(dist <= ATT_HALF, -slope * dist.astype(F32), NEG_INF)

    def body(it, carry):
        r = lax.shift_right_logical(it, n_blocks.bit_length() - 1)
        i = it & (n_blocks - 1)
        q0 = pl.multiple_of(i * ATT_QB, ATT_QB)
        k0 = pl.multiple_of(jnp.clip(i * ATT_QB - ATT_HALF, 0, seq_len - ATT_KW), ATT_HALF)
        q2 = q_ref[0, r, pl.ds(q0, ATT_QB), :] * jnp.asarray(ATT_HEAD_DIM ** -0.5, BF16)
        zero = jnp.zeros_like(q2)
        qs = jnp.concatenate([jnp.where(is_a, q2, zero), jnp.where(is_a, zero, q2)], axis=0)
        kk = k_ref[0, r, pl.ds(k0, ATT_KW), :]
        vv = v_ref[0, r, pl.ds(k0, ATT_KW), :]
        s = lax.dot_general(qs, kk, (((1,), (1,)), ((), ())), preferred_element_type=F32)
        s = s + bias_ref[lax.shift_right_logical(q0 - k0, ATT_HALF.bit_length() - 1)]
        m = jnp.max(s, axis=-1, keepdims=True)
        p = jnp.exp(s - m)
        l = jnp.sum(p, axis=-1, keepdims=True)
        pv = jnp.dot(p.astype(BF16), vv, preferred_element_type=F32)
        o = pv * (1.0 / l)
        lse = jnp.broadcast_to(m + jnp.log(l), (2 * ATT_QB, LANES))
        o_ref[0, r, pl.ds(q0, ATT_QB), :] = jnp.where(is_a, o[:ATT_QB], o[ATT_QB:])
        lse_ref[0, r, pl.ds(q0, ATT_QB), :] = jnp.where(is_a, lse[:ATT_QB], lse[ATT_QB:])
        return carry

    lax.fori_loop(0, dil * n_blocks, body, 0, unroll=ATT_UNROLL)


def _attention_group(qkv, slopes, *, dil):
    batch, d, seq_len, _ = qkv.shape
    n_pairs = ATT_OUT // LANES

    def spec(col0):
        return pl.BlockSpec((1, d, seq_len, LANES), lambda sp, b: (b, 0, 0, col0 + sp))

    out = jax.ShapeDtypeStruct((batch, d, seq_len, ATT_OUT), F32)
    return pl.pallas_call(
        functools.partial(_attn_kernel, seq_len=seq_len, dil=dil),
        grid=(n_pairs, batch),
        in_specs=[pl.BlockSpec(memory_space=pltpu.SMEM), spec(0), spec(n_pairs), spec(2 * n_pairs)],
        out_specs=[spec(0), spec(0)],
        out_shape=[out, out],
        scratch_shapes=[pltpu.VMEM((3, 2 * ATT_QB, ATT_KW), F32)],
        compiler_params=_cparams(("arbitrary", "arbitrary")),
        name=f"attn_d{dil}",
    )(slopes, qkv, qkv, qkv)


def _merge_kernel(o0_ref, l0_ref, o1_ref, l1_ref, o2_ref, l2_ref, att_ref, so_ref, sl_ref, *, tm):
    outs = [o0_ref[0, 0]]
    lses = [l0_ref[0, 0]]
    for d, o_ref, l_ref in ((DILATIONS[1], o1_ref, l1_ref), (DILATIONS[2], o2_ref, l2_ref)):
        for r in range(d):
            for cb in range(ATT_OUT // LANES):
                cols = slice(cb * LANES, (cb + 1) * LANES)
                so_ref[cb, pl.ds(r, tm // d, stride=d), :] = o_ref[0, r, :, cols]
                sl_ref[cb, pl.ds(r, tm // d, stride=d), :] = l_ref[0, r, :, cols]
        outs.append(jnp.concatenate([so_ref[cb] for cb in range(ATT_OUT // LANES)], axis=1))
        lses.append(jnp.concatenate([sl_ref[cb] for cb in range(ATT_OUT // LANES)], axis=1))
    mx = jnp.maximum(jnp.maximum(lses[0], lses[1]), lses[2])
    ws = [jnp.exp(l - mx) for l in lses]
    num = ws[0] * outs[0] + ws[1] * outs[1] + ws[2] * outs[2]
    att_ref[...] = (num / (ws[0] + ws[1] + ws[2])).astype(BF16)


def _merge(ol, *, batch, seq, tm):
    t = batch * seq
    per_b = seq // tm
    in_specs = []
    args = []
    for d, (o, l) in zip(DILATIONS, ol):
        spec = pl.BlockSpec((1, d, tm // d, ATT_OUT), lambda i: (i // per_b, 0, i % per_b, 0))
        in_specs += [spec, spec]
        args += [o, l]
    return pl.pallas_call(
        functools.partial(_merge_kernel, tm=tm),
        grid=(t // tm,),
        in_specs=in_specs,
        out_specs=pl.BlockSpec((tm, ATT_OUT), lambda i: (i, 0)),
        out_shape=jax.ShapeDtypeStruct((t, ATT_OUT), BF16),
        scratch_shapes=[pltpu.VMEM((ATT_OUT // LANES, tm, LANES), F32)] * 2,
        compiler_params=_cparams(("arbitrary",)),
        name="merge",
    )(*args)


HALO = 8
SSD_SUB = 2
SSD_ROWS = SSD_SUB * SSD_CHUNK


def _expand_rows(fac, col0, is_a):
    parts = []
    for j in range(SSD_HPG // 2):
        a = fac[:, col0 + 2 * j:col0 + 2 * j + 1]
        b = fac[:, col0 + 2 * j + 1:col0 + 2 * j + 2]
        parts.append(jnp.where(is_a, a, b))
    return jnp.concatenate(parts, axis=1)


def _split2(v):
    hi = v.astype(BF16)
    lo = (v - hi.astype(F32)).astype(BF16)
    return jnp.concatenate([hi, lo], axis=1)


def _split3(v):
    hi = v.astype(BF16)
    r1 = v - hi.astype(F32)
    mid = r1.astype(BF16)
    lo = (r1 - mid.astype(F32)).astype(BF16)
    return jnp.concatenate([hi, mid, lo], axis=1)


def _ssd_kernel(z_ref, xbc_ref, prev_ref, next_ref, dt_ref, cw_ref, cb_ref, dtb_ref, alog_ref,
                dskip_ref, nw_ref, out_ref, ext_ref, xc_ref, yf_ref, st_ref, sel_ref, *, n_blocks):
    q = SSD_CHUNK
    p = pl.program_id(1)
    c = pl.program_id(2)
    fwd = p == 0
    cc = jnp.where(fwd, c, n_blocks - 1 - c)
    row0 = pl.multiple_of(cc * SSD_ROWS, SSD_ROWS)

    @pl.when(jnp.logical_and(fwd, c == 0))
    def _():
        yf_ref[...] = jnp.zeros_like(yf_ref)

    @pl.when(c == 0)
    def _():
        st_ref[...] = jnp.zeros_like(st_ref)
        src = lax.broadcasted_iota(jnp.int32, (2 * LANES, SSD_INNER), 0) & (LANES - 1)
        dst = lax.shift_right_logical(lax.broadcasted_iota(jnp.int32, (2 * LANES, SSD_INNER), 1),
                                      SSD_HEAD_DIM.bit_length() - 1)
        sel_ref[...] = jnp.where(src == dst, 1.0, 0.0).astype(BF16)

    @pl.when(fwd)
    def _():
        ext_ref[0:HALO, :] = jnp.where(cc == 0, 0.0, prev_ref[...])
        ext_ref[HALO:HALO + SSD_ROWS, :] = xbc_ref[...]
        ext_ref[HALO + SSD_ROWS:, :] = jnp.where(cc == n_blocks - 1, 0.0, next_ref[...])
        acc = jnp.broadcast_to(cb_ref[...], (SSD_ROWS, CONV_CH))
        for k in range(SSD_CONV):
            acc = acc + cw_ref[k:k + 1, :] * ext_ref[pl.ds(HALO - SSD_CONV // 2 + k, SSD_ROWS), :]
        xc_ref[pl.ds(row0, SSD_ROWS), :] = _silu(acc)

    gn = SSD_GROUPS * SSD_STATE
    half = SSD_INNER // SSD_GROUPS
    ri = lax.broadcasted_iota(jnp.int32, (q, q), 0)
    ci = lax.broadcasted_iota(jnp.int32, (q, q), 1)
    before = (ci - ri) * jnp.where(fwd, 1, -1) <= 0
    cum_lhs = jnp.concatenate([before.astype(BF16), jnp.ones((q, q), BF16)], axis=0)
    is_a = lax.broadcasted_iota(jnp.int32, (q, LANES), 1) < SSD_HEAD_DIM
    neg_a = -jnp.exp(alog_ref[...])

    for k in range(SSD_SUB):
        off = pl.multiple_of(jnp.where(fwd, k, SSD_SUB - 1 - k) * q, q)
        rows = pl.ds(row0 + off, q)
        xs = xc_ref[rows, 0:SSD_INNER]

        dt_all = _softplus(dt_ref[pl.ds(off, q), :] + dtb_ref[...])
        dta_all = dt_all * neg_a
        dt_d = jnp.where(fwd, dt_all, pltpu.roll(dt_all, LANES - SSD_HEADS, 1))
        dta_d = jnp.where(fwd, dta_all, pltpu.roll(dta_all, LANES - SSD_HEADS, 1))
        sums = jnp.dot(cum_lhs, _split3(dta_d), preferred_element_type=F32)
        sums = sums[:, 0:LANES] + sums[:, LANES:2 * LANES] + sums[:, 2 * LANES:]
        cum = sums[:q]
        total = sums[q:]
        wide = jnp.dot(jnp.concatenate([_split2(jnp.exp(cum)), _split2(jnp.exp(total - cum) * dt_d)], axis=0),
                       sel_ref[...], preferred_element_type=F32)
        e_cum_w = wide[:q]
        d_state_w = wide[q:]
        d_chunk = jnp.exp(total[0:8])
        log_dt_t = jnp.log(dt_d).T
        shifted_t = cum.T - log_dt_t

        ys = []
        for g in range(SSD_GROUPS):
            col0 = g * SSD_HPG
            gcols = slice(g * half, (g + 1) * half)
            xg = xs[:, gcols]
            bg = xc_ref[rows, SSD_INNER + g * SSD_STATE:SSD_INNER + (g + 1) * SSD_STATE]
            cg = xc_ref[rows, SSD_INNER + gn + g * SSD_STATE:SSD_INNER + gn + (g + 1) * SSD_STATE]
            xg16 = xg.astype(BF16)
            bg16 = bg.astype(BF16)
            cg16 = cg.astype(BF16)
            cbm = lax.dot_general(cg16, bg16, (((1,), (1,)), ((), ())), preferred_element_type=F32)
            cbm = jnp.where(before, cbm, 0.0)
            y_parts = []
            for j in range(SSD_HPG // 2):
                ms = []
                for h in (2 * j, 2 * j + 1):
                    col = col0 + h
                    diff = cum[:, col:col + 1] - shifted_t[col:col + 1, :]
                    ms.append((cbm * jnp.exp(jnp.minimum(diff, log_dt_t[col:col + 1, :]))).astype(BF16))
                xp = xg16[:, j * LANES:(j + 1) * LANES]
                zero = jnp.zeros_like(xp)
                xdiag = jnp.concatenate([jnp.where(is_a, xp, zero), jnp.where(is_a, zero, xp)], axis=0)
                y_parts.append(jnp.dot(jnp.concatenate(ms, axis=1), xdiag, preferred_element_type=F32))
            y_diag = jnp.concatenate(y_parts, axis=1)
            st = st_ref[g]
            y_off = jnp.dot(cg16, st.astype(BF16), preferred_element_type=F32) * e_cum_w[:, gcols]
            xd = (xg * d_state_w[:, gcols]).astype(BF16)
            new = jnp.dot(bg.T.astype(BF16), xd, preferred_element_type=F32)
            keep = _expand_rows(d_chunk, col0, is_a[0:8])
            st_ref[g] = (st.reshape(SSD_STATE // 8, 8, -1) * keep[None]).reshape(st.shape) + new
            ys.append(y_diag + y_off)
        y = jnp.concatenate(ys, axis=1)
        yf_ref[rows, :] += y

    @pl.when(jnp.logical_not(fwd))
    def _():
        rows = pl.ds(row0, SSD_ROWS)
        yt = (yf_ref[rows, :] + xc_ref[rows, 0:SSD_INNER] * dskip_ref[...]) * _silu(z_ref[...])
        for g in range(SSD_GROUPS):
            v = yt[:, g * half:(g + 1) * half]
            ms = jnp.mean(v * v, axis=-1, keepdims=True)
            out_ref[:, g * half:(g + 1) * half] = (
                v * lax.rsqrt(ms + RMS_EPS) * nw_ref[:, g * half:(g + 1) * half]).astype(BF16)


def _ssd(zux, dt, cw, cb, dtb, alog, dskip, nw, *, batch, seq):
    t = batch * seq
    rows = SSD_ROWS
    n_blocks = seq // rows
    hb = rows // HALO

    def row_blk(b, p, c):
        return b * n_blocks + jnp.where(p == 0, c, n_blocks - 1 - c)

    def conv_blk(b, p, c):
        return b * n_blocks + jnp.where(p == 0, c, n_blocks - 1)

    def prev_blk(b, p, c):
        return jnp.maximum(conv_blk(b, p, c) * hb - 1, b * n_blocks * hb)

    def next_blk(b, p, c):
        return jnp.minimum((conv_blk(b, p, c) + 1) * hb, (b + 1) * n_blocks * hb - 1)

    def z_blk(b, p, c):
        return jnp.where(p == 0, b * n_blocks + n_blocks - 1, row_blk(b, p, c))

    def const(shape):
        return pl.BlockSpec(shape, lambda b, p, c: (0, 0))

    return pl.pallas_call(
        functools.partial(_ssd_kernel, n_blocks=n_blocks),
        grid=(batch, 2, n_blocks),
        in_specs=[
            pl.BlockSpec((rows, SSD_INNER), lambda b, p, c: (z_blk(b, p, c), 0)),
            pl.BlockSpec((rows, CONV_CH), lambda b, p, c: (conv_blk(b, p, c), 1)),
            pl.BlockSpec((HALO, CONV_CH), lambda b, p, c: (prev_blk(b, p, c), 1)),
            pl.BlockSpec((HALO, CONV_CH), lambda b, p, c: (next_blk(b, p, c), 1)),
            pl.BlockSpec((rows, LANES), lambda b, p, c: (row_blk(b, p, c), 0)),
            const((8, CONV_CH)), const((1, CONV_CH)), const((1, LANES)), const((1, LANES)),
            const((1, SSD_INNER)), const((1, SSD_INNER)),
        ],
        out_specs=pl.BlockSpec((rows, SSD_INNER), lambda b, p, c: (z_blk(b, p, c), 0)),
        out_shape=jax.ShapeDtypeStruct((t, SSD_INNER), BF16),
        scratch_shapes=[
            pltpu.VMEM((rows + 2 * HALO, CONV_CH), F32),
            pltpu.VMEM((seq, CONV_CH), F32),
            pltpu.VMEM((seq, SSD_INNER), F32),
            pltpu.VMEM((SSD_GROUPS, SSD_STATE, SSD_INNER // SSD_GROUPS), F32),
            pltpu.VMEM((2 * LANES, SSD_INNER), BF16),
        ],
        compiler_params=_cparams(("arbitrary", "arbitrary", "arbitrary")),
        name="ssd",
    )(zux, zux, zux, zux, dt, cw, cb, dtb, alog, dskip, nw)


POOL_PAD = 16
POOL_ROWS = 512


def _pool_kernel(u_ref, w_ref, sc_ref, out_ref, pad_ref, *, seq):
    zeros = jnp.zeros((POOL_PAD, POOL_WIDTH), F32)
    pad_ref[0:POOL_PAD, :] = zeros
    pad_ref[POOL_PAD + seq:, :] = zeros
    pad_ref[POOL_PAD:POOL_PAD + seq, :] = u_ref[...]
    for r0 in range(0, seq, POOL_ROWS):
        pos = r0 + lax.broadcasted_iota(jnp.int32, (POOL_ROWS, POOL_GROUP), 0)
        for g, w in enumerate(POOL_WINDOWS):
            cols = slice(g * POOL_GROUP, (g + 1) * POOL_GROUP)
            u = pad_ref[POOL_PAD + r0:POOL_PAD + r0 + POOL_ROWS, cols]
            s = None
            for j in range(-(w // 2), w // 2):
                term = pad_ref[POOL_PAD + r0 + j:POOL_PAD + r0 + j + POOL_ROWS, cols]
                s = term if s is None else s + term
            cnt = jnp.minimum(pos + w // 2, seq) - jnp.maximum(pos - w // 2, 0)
            mean = s / cnt.astype(F32)
            y = jnp.dot((mean - u).astype(BF16), w_ref[g], preferred_element_type=F32)
            out_ref[r0:r0 + POOL_ROWS, cols] = (y * sc_ref[:, cols]).astype(BF16)


def _pool(zux, pw, psc, *, batch, seq):
    t = batch * seq
    return pl.pallas_call(
        functools.partial(_pool_kernel, seq=seq),
        scratch_shapes=[pltpu.VMEM((seq + 2 * POOL_PAD, POOL_WIDTH), F32)],
        grid=(batch,),
        in_specs=[
            pl.BlockSpec((seq, POOL_WIDTH), lambda b: (b, SSD_INNER // POOL_WIDTH)),
            pl.BlockSpec((len(POOL_WINDOWS), POOL_GROUP, POOL_GROUP), lambda b: (0, 0, 0)),
            pl.BlockSpec((1, POOL_WIDTH), lambda b: (0, 0)),
        ],
        out_specs=pl.BlockSpec((seq, POOL_WIDTH), lambda b: (b, 0)),
        out_shape=jax.ShapeDtypeStruct((t, POOL_WIDTH), BF16),
        compiler_params=_cparams(("arbitrary",)),
        name="pool",
    )(zux, pw, psc)


def _outproj_kernel(x_ref, a_ref, s_ref, p_ref, wa_ref, ws_ref, wp_ref, o_ref):
    acc = jnp.dot(a_ref[...], wa_ref[...], preferred_element_type=F32)
    acc = acc + jnp.dot(s_ref[...], ws_ref[...], preferred_element_type=F32)
    acc = acc + jnp.dot(p_ref[...], wp_ref[...], preferred_element_type=F32)
    o_ref[...] = x_ref[...] + acc


def _outproj(x2, att, ssd, pool, wa, ws, wp, *, tm, tn):
    t = x2.shape[0]
    return pl.pallas_call(
        _outproj_kernel,
        grid=(D_MODEL // tn, t // tm),
        in_specs=[
            pl.BlockSpec((tm, tn), lambda j, i: (i, j)),
            pl.BlockSpec((tm, ATT_OUT), lambda j, i: (i, 0)),
            pl.BlockSpec((tm, SSD_INNER), lambda j, i: (i, 0)),
            pl.BlockSpec((tm, POOL_WIDTH), lambda j, i: (i, 0)),
            pl.BlockSpec((ATT_OUT, tn), lambda j, i: (0, j)),
            pl.BlockSpec((SSD_INNER, tn), lambda j, i: (0, j)),
            pl.BlockSpec((POOL_WIDTH, tn), lambda j, i: (0, j)),
        ],
        out_specs=pl.BlockSpec((tm, tn), lambda j, i: (i, j)),
        out_shape=jax.ShapeDtypeStruct((t, D_MODEL), F32),
        compiler_params=_cparams(("arbitrary", "arbitrary")),
        name="outproj",
    )(x2, att, ssd, pool, wa, ws, wp)


def _ffn_kernel(x_ref, nw_ref, wg_ref, wu_ref, wd_ref, fw_ref, o_ref, h_ref, *, n_ff, final):
    j = pl.program_id(1)

    @pl.when(j == 0)
    def _():
        x = x_ref[...]
        ms = jnp.mean(x * x, axis=-1, keepdims=True)
        h_ref[...] = (x * lax.rsqrt(ms + RMS_EPS) * nw_ref[...]).astype(BF16)
        o_ref[...] = x

    h = h_ref[...]
    gate = jnp.dot(h, wg_ref[...], preferred_element_type=F32)
    up = jnp.dot(h, wu_ref[...], preferred_element_type=F32)
    act = (_silu(gate) * up).astype(BF16)
    o_ref[...] += jnp.dot(act, wd_ref[...], preferred_element_type=F32)

    if final:
        @pl.when(j == n_ff - 1)
        def _():
            y = o_ref[...]
            ms = jnp.mean(y * y, axis=-1, keepdims=True)
            o_ref[...] = y * lax.rsqrt(ms + RMS_EPS) * fw_ref[...]


def _ffn(x2, nw, wg, wu, wd, fw, *, tm, tf, final):
    t = x2.shape[0]
    n_ff = D_FF // tf
    return pl.pallas_call(
        functools.partial(_ffn_kernel, n_ff=n_ff, final=final),
        grid=(t // tm, n_ff),
        in_specs=[
            pl.BlockSpec((tm, D_MODEL), lambda i, j: (i, 0)),
            pl.BlockSpec((1, D_MODEL), lambda i, j: (0, 0)),
            pl.BlockSpec((D_MODEL, tf), lambda i, j: (0, j)),
            pl.BlockSpec((D_MODEL, tf), lambda i, j: (0, j)),
            pl.BlockSpec((tf, D_MODEL), lambda i, j: (j, 0)),
            pl.BlockSpec((1, D_MODEL), lambda i, j: (0, 0)),
        ],
        out_specs=pl.BlockSpec((tm, D_MODEL), lambda i, j: (i, 0)),
        out_shape=jax.ShapeDtypeStruct((t, D_MODEL), F32),
        scratch_shapes=[pltpu.VMEM((tm, D_MODEL), BF16)],
        compiler_params=_cparams(("arbitrary", "arbitrary")),
        name="ffn_final" if final else "ffn",
    )(x2, nw, wg, wu, wd, fw)


def _pack_in_weights(w_in):
    n_layers = w_in.shape[0]
    z0 = 3 * ATT_WIDTH
    xbc0 = z0 + SSD_INNER
    dt0 = xbc0 + CONV_CH
    u0 = dt0 + 2 * SSD_HEADS
    qkv = w_in[:, :, :z0].reshape(n_layers, D_MODEL, 3, N_DIL, ATT_OUT)
    wqkv = qkv.transpose(0, 1, 3, 2, 4).reshape(n_layers, D_MODEL, z0).astype(BF16)
    wzux = jnp.concatenate([w_in[:, :, z0:xbc0], w_in[:, :, u0:u0 + POOL_WIDTH], w_in[:, :, xbc0:dt0]],
                           axis=2).astype(BF16)
    wdt = jnp.pad(w_in[:, :, dt0:u0], ((0, 0), (0, 0), (0, LANES - 2 * SSD_HEADS))).astype(BF16)
    return wqkv, wzux, wdt


def _pad_lanes(v):
    return jnp.pad(v.reshape(1, -1).astype(F32), ((0, 0), (0, LANES - v.size)))


def _layer(x2, slopes, p, fw, *, batch, seq, final):
    q0, q1, q2, zux, dt = _inproj(x2, p["norm1_w"].reshape(1, -1), *p["w_in_packed"], p["layer"],
                                  batch=batch, seq=seq, tm=IN_TM)
    ol = [_attention_group(qkv, slopes[g], dil=d) for g, (d, qkv) in enumerate(zip(DILATIONS, (q0, q1, q2)))]
    att = _merge(ol, batch=batch, seq=seq, tm=512)
    cw = jnp.pad(p["conv_w"].astype(F32), ((0, 8 - SSD_CONV), (0, 0)))
    ssd = _ssd(zux, dt, cw, p["conv_b"].reshape(1, -1).astype(F32), _pad_lanes(p["dt_bias"]),
               _pad_lanes(p["a_log"]), jnp.repeat(p["d_skip"].astype(F32), SSD_HEAD_DIM).reshape(1, -1),
               p["ssd_norm_w"].reshape(1, -1).astype(F32), batch=batch, seq=seq)
    pool = _pool(zux, p["pool_w"].astype(BF16), p["pool_scale"].reshape(1, -1).astype(F32),
                 batch=batch, seq=seq)
    wo = p["w_out"].astype(BF16)
    x2 = _outproj(x2, att, ssd, pool, wo[:ATT_OUT], wo[ATT_OUT:ATT_OUT + SSD_INNER],
                  wo[ATT_OUT + SSD_INNER:], tm=1024, tn=OUT_TN)
    return _ffn(x2, p["norm2_w"].reshape(1, -1), p["w_gate"].astype(BF16), p["w_up"].astype(BF16),
                p["w_down"].astype(BF16), fw.reshape(1, -1), tm=FFN_TM, tf=FFN_TF, final=final)


def kernel(x, norm1_w, w_in, conv_w, conv_b, dt_bias, a_log, d_skip, ssd_norm_w, pool_w, pool_scale,
           w_out, norm2_w, w_gate, w_up, w_down, final_norm_w):
    batch, seq, _ = x.shape
    depth = w_in.shape[0]
    k = jnp.arange(1, ATT_HEADS + 1, dtype=F32)
    slopes = (2.0 ** (-8.0 * k / ATT_HEADS)).reshape(N_DIL, ATT_SLOTS)
    x2 = x.reshape(batch * seq, D_MODEL)
    w_in_packed = _pack_in_weights(w_in)
    for l in range(depth):
        p = dict(norm1_w=norm1_w[l], w_in_packed=w_in_packed, layer=l, conv_w=conv_w[l], conv_b=conv_b[l], dt_bias=dt_bias[l],
                 a_log=a_log[l], d_skip=d_skip[l], ssd_norm_w=ssd_norm_w[l], pool_w=pool_w[l],
                 pool_scale=pool_scale[l], w_out=w_out[l], norm2_w=norm2_w[l], w_gate=w_gate[l],
                 w_up=w_up[l], w_down=w_down[l])
        x2 = _layer(x2, slopes, p, final_norm_w, batch=batch, seq=seq, final=(l == depth - 1))
    return x2.reshape(batch, seq, D_MODEL)
```

```python
import functools
import math

import jax
import jax.numpy as jnp
from jax import lax
from jax.experimental import pallas as pl
from jax.experimental.pallas import tpu as pltpu

F32 = jnp.float32
BF16 = jnp.bfloat16

D_MODEL = 2048
ATT_HEAD_DIM = 64
ATT_SLOTS = 8
DILATIONS = (1, 4, 16)
ATT_HALF = 64
N_DIL = len(DILATIONS)
ATT_HEADS = ATT_SLOTS * N_DIL
ATT_WIDTH = ATT_HEADS * ATT_HEAD_DIM
ATT_OUT = ATT_SLOTS * ATT_HEAD_DIM
SSD_HEAD_DIM = 64
SSD_INNER = D_MODEL // 2
SSD_HEADS = SSD_INNER // SSD_HEAD_DIM
SSD_GROUPS = 2
SSD_HPG = SSD_HEADS // SSD_GROUPS
SSD_STATE = 128
SSD_CONV = 5
SSD_CHUNK = 128
CONV_CH = SSD_INNER + 2 * SSD_GROUPS * SSD_STATE
POOL_WINDOWS = (2, 4, 8, 16)
POOL_GROUP = 128
POOL_WIDTH = POOL_GROUP * len(POOL_WINDOWS)
MIX_WIDTH = ATT_OUT + SSD_INNER + POOL_WIDTH
D_FF = 5632
RMS_EPS = 1e-6
NEG_INF = -1e30

LANES = 128
IN_TILE = 768
IN_TILES_PER_GROUP = 2
N_IN_TILES = 5 * IN_TILES_PER_GROUP
IN_TM = 1024
VMEM_LIMIT = 56 * 1024 * 1024
OUT_TN = 1024
FFN_TM = 1024
FFN_TF = 512


def _cparams(sem):
    return pltpu.CompilerParams(dimension_semantics=sem, vmem_limit_bytes=VMEM_LIMIT)


def _silu(x):
    return x * (1.0 / (1.0 + jnp.exp(-x)))


def _softplus(x):
    return jnp.maximum(x, 0.0) + jnp.log(1.0 + jnp.exp(-jnp.abs(x)))


def _inproj_kernel(x_ref, nw_ref, w_ref, wdt_ref, q0_ref, q1_ref, q2_ref, zux_ref, dt_ref,
                   h_ref, stage_ref, *, tm):
    j = pl.program_id(1)
    jg = j // IN_TILES_PER_GROUP

    @pl.when(j == 0)
    def _():
        x = x_ref[...]
        ms = jnp.mean(x * x, axis=-1, keepdims=True)
        h = x * lax.rsqrt(ms + RMS_EPS) * nw_ref[...]
        h_ref[...] = h.astype(BF16)
        dt_ref[...] = jnp.dot(h_ref[...], wdt_ref[...], preferred_element_type=F32)

    def proj():
        return jnp.dot(h_ref[...], w_ref[...], preferred_element_type=F32)

    @pl.when(jg == 0)
    def _():
        q0_ref[0, 0] = proj().astype(BF16)

    for g, (d, out_ref) in enumerate(zip(DILATIONS, (q0_ref, q1_ref, q2_ref))):
        if g == 0:
            continue

        @pl.when(jg == g)
        def _(d=d, out_ref=out_ref):
            res = proj()
            for cb in range(IN_TILE // LANES):
                stage_ref[cb] = res[:, cb * LANES:(cb + 1) * LANES]
            for r in range(d):
                for cb in range(IN_TILE // LANES):
                    out_ref[0, r, :, cb * LANES:(cb + 1) * LANES] = (
                        stage_ref[cb, pl.ds(r, tm // d, stride=d), :].astype(BF16))

    @pl.when(jg >= N_DIL)
    def _():
        zux_ref[...] = proj()


def _inproj(x2, nw, wcat, wdt, *, batch, seq, tm):
    t = x2.shape[0]
    per_b = seq // tm
    grid = (t // tm, N_IN_TILES)
    tpg = IN_TILES_PER_GROUP

    def qspec(g, d):
        return pl.BlockSpec((1, d, tm // d, IN_TILE),
                            lambda i, j: (i // per_b, 0, i % per_b, jnp.clip(j - g * tpg, 0, tpg - 1)))

    out_shape = [jax.ShapeDtypeStruct((batch, d, seq // d, tpg * IN_TILE), BF16) for d in DILATIONS]
    out_shape += [jax.ShapeDtypeStruct((t, 2 * tpg * IN_TILE), F32), jax.ShapeDtypeStruct((t, LANES), F32)]
    return pl.pallas_call(
        functools.partial(_inproj_kernel, tm=tm),
        grid=grid,
        in_specs=[
            pl.BlockSpec((tm, D_MODEL), lambda i, j: (i, 0)),
            pl.BlockSpec((1, D_MODEL), lambda i, j: (0, 0)),
            pl.BlockSpec((D_MODEL, IN_TILE), lambda i, j: (0, j)),
            pl.BlockSpec((D_MODEL, LANES), lambda i, j: (0, 0)),
        ],
        out_specs=[qspec(g, d) for g, d in enumerate(DILATIONS)] + [
            pl.BlockSpec((tm, IN_TILE), lambda i, j: (i, jnp.maximum(j - N_DIL * tpg, 0))),
            pl.BlockSpec((tm, LANES), lambda i, j: (i, 0)),
        ],
        out_shape=out_shape,
        scratch_shapes=[pltpu.VMEM((tm, D_MODEL), BF16), pltpu.VMEM((IN_TILE // LANES, tm, LANES), F32)],
        compiler_params=_cparams(("arbitrary", "arbitrary")),
        name="inproj",
    )(x2, nw, wcat, wdt)


ATT_QB = 128
ATT_KW = 256
ATT_UNROLL = 8


def _attn_kernel(slopes_ref, q_ref, k_ref, v_ref, o_ref, lse_ref, bias_ref, *, seq_len, dil):
    sp = pl.program_id(0)
    is_a = lax.broadcasted_iota(jnp.int32, (ATT_QB, LANES), 1) < ATT_HEAD_DIM
    n_blocks = seq_len // ATT_QB

    @pl.when(pl.program_id(1) == 0)
    def _():
        slope_a = slopes_ref[2 * sp] * float(dil)
        slope_b = slopes_ref[2 * sp + 1] * float(dil)
        rows = lax.broadcasted_iota(jnp.int32, (2 * ATT_QB, ATT_KW), 0)
        cols = lax.broadcasted_iota(jnp.int32, (2 * ATT_QB, ATT_KW), 1)
        rel_base = cols - (rows & (ATT_QB - 1))
        slope = jnp.where(rows < ATT_QB, slope_a, slope_b)
        for t in range(3):
            dist = jnp.abs(rel_base - t * ATT_HALF)
            bias_ref[t] = jnp.where(dist <= ATT_HALF, -slope * dist.astype(F32), NEG_INF)

    def body(it, carry):
        r = lax.shift_right_logical(it, n_blocks.bit_length() - 1)
        i = it & (n_blocks - 1)
        q0 = pl.multiple_of(i * ATT_QB, ATT_QB)
        k0 = pl.multiple_of(jnp.clip(i * ATT_QB - ATT_HALF, 0, seq_len - ATT_KW), ATT_HALF)
        q2 = q_ref[0, r, pl.ds(q0, ATT_QB), :] * jnp.asarray(ATT_HEAD_DIM ** -0.5, BF16)
        zero = jnp.zeros_like(q2)
        qs = jnp.concatenate([jnp.where(is_a, q2, zero), jnp.where(is_a, zero, q2)], axis=0)
        kk = k_ref[0, r, pl.ds(k0, ATT_KW), :]
        vv = v_ref[0, r, pl.ds(k0, ATT_KW), :]
        s = lax.dot_general(qs, kk, (((1,), (1,)), ((), ())), preferred_element_type=F32)
        s = s + bias_ref[lax.shift_right_logical(q0 - k0, ATT_HALF.bit_length() - 1)]
        m = jnp.max(s, axis=-1, keepdims=True)
        p = jnp.exp(s - m)
        l = jnp.sum(p, axis=-1, keepdims=True)
        pv = jnp.dot(p.astype(BF16), vv, preferred_element_type=F32)
        o = pv * (1.0 / l)
        lse = jnp.broadcast_to(m + jnp.log(l), (2 * ATT_QB, LANES))
        o_ref[0, r, pl.ds(q0, ATT_QB), :] = jnp.where(is_a, o[:ATT_QB], o[ATT_QB:]).astype(BF16)
        lse_ref[0, r, pl.ds(q0, ATT_QB), :] = jnp.where(is_a, lse[:ATT_QB], lse[ATT_QB:])
        return carry

    lax.fori_loop(0, dil * n_blocks, body, 0, unroll=ATT_UNROLL)


def _attention_group(qkv, slopes, *, dil):
    batch, d, seq_len, _ = qkv.shape
    n_pairs = ATT_OUT // LANES

    def spec(col0):
        return pl.BlockSpec((1, d, seq_len, LANES), lambda sp, b: (b, 0, 0, col0 + sp))

    shape = (batch, d, seq_len, ATT_OUT)
    return pl.pallas_call(
        functools.partial(_attn_kernel, seq_len=seq_len, dil=dil),
        grid=(n_pairs, batch),
        in_specs=[pl.BlockSpec(memory_space=pltpu.SMEM), spec(0), spec(n_pairs), spec(2 * n_pairs)],
        out_specs=[spec(0), spec(0)],
        out_shape=[jax.ShapeDtypeStruct(shape, BF16), jax.ShapeDtypeStruct(shape, F32)],
        scratch_shapes=[pltpu.VMEM((3, 2 * ATT_QB, ATT_KW), F32)],
        compiler_params=_cparams(("arbitrary", "arbitrary")),
        name=f"attn_d{dil}",
    )(slopes, qkv, qkv, qkv)


def _merge_kernel(o0_ref, l0_ref, o1_ref, l1_ref, o2_ref, l2_ref, att_ref, so_ref, sl_ref, *, tm):
    outs = [o0_ref[0, 0].astype(F32)]
    lses = [l0_ref[0, 0]]
    for d, o_ref, l_ref in ((DILATIONS[1], o1_ref, l1_ref), (DILATIONS[2], o2_ref, l2_ref)):
        for r in range(d):
            for cb in range(ATT_OUT // LANES):
                cols = slice(cb * LANES, (cb + 1) * LANES)
                so_ref[cb, pl.ds(r, tm // d, stride=d), :] = o_ref[0, r, :, cols].astype(F32)
                sl_ref[cb, pl.ds(r, tm // d, stride=d), :] = l_ref[0, r, :, cols]
        outs.append(jnp.concatenate([so_ref[cb] for cb in range(ATT_OUT // LANES)], axis=1))
        lses.append(jnp.concatenate([sl_ref[cb] for cb in range(ATT_OUT // LANES)], axis=1))
    mx = jnp.maximum(jnp.maximum(lses[0], lses[1]), lses[2])
    ws = [jnp.exp(l - mx) for l in lses]
    num = ws[0] * outs[0] + ws[1] * outs[1] + ws[2] * outs[2]
    att_ref[...] = (num / (ws[0] + ws[1] + ws[2])).astype(BF16)


def _merge(ol, *, batch, seq, tm):
    t = batch * seq
    per_b = seq // tm
    in_specs = []
    args = []
    for d, (o, l) in zip(DILATIONS, ol):
        spec = pl.BlockSpec((1, d, tm // d, ATT_OUT), lambda i: (i // per_b, 0, i % per_b, 0))
        in_specs += [spec, spec]
        args += [o, l]
    return pl.pallas_call(
        functools.partial(_merge_kernel, tm=tm),
        grid=(t // tm,),
        in_specs=in_specs,
        out_specs=pl.BlockSpec((tm, ATT_OUT), lambda i: (i, 0)),
        out_shape=jax.ShapeDtypeStruct((t, ATT_OUT), BF16),
        scratch_shapes=[pltpu.VMEM((ATT_OUT // LANES, tm, LANES), F32)] * 2,
        compiler_params=_cparams(("arbitrary",)),
        name="merge",
    )(*args)


HALO = 8
SSD_SUB = 2
SSD_ROWS = SSD_SUB * SSD_CHUNK


def _expand_rows(fac, col0, is_a):
    parts = []
    for j in range(SSD_HPG // 2):
        a = fac[:, col0 + 2 * j:col0 + 2 * j + 1]
        b = fac[:, col0 + 2 * j + 1:col0 + 2 * j + 2]
        parts.append(jnp.where(is_a, a, b))
    return jnp.concatenate(parts, axis=1)


def _split2(v):
    hi = v.astype(BF16)
    lo = (v - hi.astype(F32)).astype(BF16)
    return jnp.concatenate([hi, lo], axis=1)


def _split3(v):
    hi = v.astype(BF16)
    r1 = v - hi.astype(F32)
    mid = r1.astype(BF16)
    lo = (r1 - mid.astype(F32)).astype(BF16)
    return jnp.concatenate([hi, mid, lo], axis=1)


def _ssd_kernel(z_ref, xbc_ref, prev_ref, next_ref, dt_ref, cw_ref, cb_ref, dtb_ref, alog_ref,
                dskip_ref, nw_ref, out_ref, ext_ref, xc_ref, yf_ref, st_ref, sel_ref, *, n_blocks):
    q = SSD_CHUNK
    p = pl.program_id(1)
    c = pl.program_id(2)
    fwd = p == 0
    cc = jnp.where(fwd, c, n_blocks - 1 - c)
    row0 = pl.multiple_of(cc * SSD_ROWS, SSD_ROWS)

    @pl.when(jnp.logical_and(fwd, c == 0))
    def _():
        yf_ref[...] = jnp.zeros_like(yf_ref)

    @pl.when(c == 0)
    def _():
        st_ref[...] = jnp.zeros_like(st_ref)
        src = lax.broadcasted_iota(jnp.int32, (2 * LANES, SSD_INNER), 0) & (LANES - 1)
        dst = lax.shift_right_logical(lax.broadcasted_iota(jnp.int32, (2 * LANES, SSD_INNER), 1),
                                      SSD_HEAD_DIM.bit_length() - 1)
        sel_ref[...] = jnp.where(src == dst, 1.0, 0.0).astype(BF16)

    @pl.when(fwd)
    def _():
        for cb in range(CONV_CH // LANES):
            cols = slice(cb * LANES, (cb + 1) * LANES)
            ext_ref[cb, 0:HALO, :] = jnp.where(cc == 0, 0.0, prev_ref[:, cols])
            ext_ref[cb, HALO:HALO + SSD_ROWS, :] = xbc_ref[:, cols]
            ext_ref[cb, HALO + SSD_ROWS:, :] = jnp.where(cc == n_blocks - 1, 0.0, next_ref[:, cols])
            acc = jnp.broadcast_to(cb_ref[:, cols], (SSD_ROWS, LANES))
            for k in range(SSD_CONV):
                acc = acc + cw_ref[k:k + 1, cols] * ext_ref[cb, pl.ds(HALO - SSD_CONV // 2 + k, SSD_ROWS), :]
            xc_ref[pl.ds(row0, SSD_ROWS), cols] = _silu(acc)

    gn = SSD_GROUPS * SSD_STATE
    half = SSD_INNER // SSD_GROUPS
    ri = lax.broadcasted_iota(jnp.int32, (q, q), 0)
    ci = lax.broadcasted_iota(jnp.int32, (q, q), 1)
    before = (ci - ri) * jnp.where(fwd, 1, -1) <= 0
    cum_lhs = jnp.concatenate([before.astype(BF16), jnp.ones((q, q), BF16)], axis=0)
    is_a = lax.broadcasted_iota(jnp.int32, (q, LANES), 1) < SSD_HEAD_DIM
    neg_a = -jnp.exp(alog_ref[...])

    for k in range(SSD_SUB):
        off = pl.multiple_of(jnp.where(fwd, k, SSD_SUB - 1 - k) * q, q)
        rows = pl.ds(row0 + off, q)
        xs = xc_ref[rows, 0:SSD_INNER]

        dt_all = _softplus(dt_ref[pl.ds(off, q), :] + dtb_ref[...])
        dta_all = dt_all * neg_a
        dt_d = jnp.where(fwd, dt_all, pltpu.roll(dt_all, LANES - SSD_HEADS, 1))
        dta_d = jnp.where(fwd, dta_all, pltpu.roll(dta_all, LANES - SSD_HEADS, 1))
        sums = jnp.dot(cum_lhs, _split3(dta_d), preferred_element_type=F32)
        sums = sums[:, 0:LANES] + sums[:, LANES:2 * LANES] + sums[:, 2 * LANES:]
        cum = sums[:q]
        total = sums[q:]
        wide = jnp.dot(jnp.concatenate([_split2(jnp.exp(cum)), _split2(jnp.exp(total - cum) * dt_d)], axis=0),
                       sel_ref[...], preferred_element_type=F32)
        e_cum_w = wide[:q]
        d_state_w = wide[q:]
        d_chunk = jnp.exp(total[0:8])
        log_dt_t = jnp.log(dt_d).T
        shifted_t = cum.T - log_dt_t

        ys = []
        for g in range(SSD_GROUPS):
            col0 = g * SSD_HPG
            gcols = slice(g * half, (g + 1) * half)
            xg = xs[:, gcols]
            bg = xc_ref[rows, SSD_INNER + g * SSD_STATE:SSD_INNER + (g + 1) * SSD_STATE]
            cg = xc_ref[rows, SSD_INNER + gn + g * SSD_STATE:SSD_INNER + gn + (g + 1) * SSD_STATE]
            xg16 = xg.astype(BF16)
            bg16 = bg.astype(BF16)
            cg16 = cg.astype(BF16)
            cbm = lax.dot_general(cg16, bg16, (((1,), (1,)), ((), ())), preferred_element_type=F32)
            cbm = jnp.where(before, cbm, 0.0)
            y_parts = []
            for j in range(SSD_HPG // 2):
                ms = []
                for h in (2 * j, 2 * j + 1):
                    col = col0 + h
                    diff = cum[:, col:col + 1] - shifted_t[col:col + 1, :]
                    ms.append((cbm * jnp.exp(jnp.minimum(diff, log_dt_t[col:col + 1, :]))).astype(BF16))
                xp = xg16[:, j * LANES:(j + 1) * LANES]
                zero = jnp.zeros_like(xp)
                xdiag = jnp.concatenate([jnp.where(is_a, xp, zero), jnp.where(is_a, zero, xp)], axis=0)
                y_parts.append(jnp.dot(jnp.concatenate(ms, axis=1), xdiag, preferred_element_type=F32))
            y_diag = jnp.concatenate(y_parts, axis=1)
            st = st_ref[g]
            y_off = jnp.dot(cg16, st.astype(BF16), preferred_element_type=F32) * e_cum_w[:, gcols]
            xd = (xg * d_state_w[:, gcols]).astype(BF16)
            new = jnp.dot(bg.T.astype(BF16), xd, preferred_element_type=F32)
            keep = _expand_rows(d_chunk, col0, is_a[0:8])
            st_ref[g] = (st.reshape(SSD_STATE // 8, 8, -1) * keep[None]).reshape(st.shape) + new
            ys.append(y_diag + y_off)
        y = jnp.concatenate(ys, axis=1)
        yf_ref[rows, :] += y

    @pl.when(jnp.logical_not(fwd))
    def _():
        rows = pl.ds(row0, SSD_ROWS)
        yt = (yf_ref[rows, :] + xc_ref[rows, 0:SSD_INNER] * dskip_ref[...]) * _silu(z_ref[...])
        for g in range(SSD_GROUPS):
            v = yt[:, g * half:(g + 1) * half]
            ms = jnp.mean(v * v, axis=-1, keepdims=True)
            out_ref[:, g * half:(g + 1) * half] = (
                v * lax.rsqrt(ms + RMS_EPS) * nw_ref[:, g * half:(g + 1) * half]).astype(BF16)


def _ssd(zux, dt, cw, cb, dtb, alog, dskip, nw, *, batch, seq):
    t = batch * seq
    rows = SSD_ROWS
    n_blocks = seq // rows
    hb = rows // HALO

    def row_blk(b, p, c):
        return b * n_blocks + jnp.where(p == 0, c, n_blocks - 1 - c)

    def conv_blk(b, p, c):
        return b * n_blocks + jnp.where(p == 0, c, n_blocks - 1)

    def prev_blk(b, p, c):
        return jnp.maximum(conv_blk(b, p, c) * hb - 1, b * n_blocks * hb)

    def next_blk(b, p, c):
        return jnp.minimum((conv_blk(b, p, c) + 1) * hb, (b + 1) * n_blocks * hb - 1)

    def z_blk(b, p, c):
        return jnp.where(p == 0, b * n_blocks + n_blocks - 1, row_blk(b, p, c))

    def const(shape):
        return pl.BlockSpec(shape, lambda b, p, c: (0, 0))

    return pl.pallas_call(
        functools.partial(_ssd_kernel, n_blocks=n_blocks),
        grid=(batch, 2, n_blocks),
        in_specs=[
            pl.BlockSpec((rows, SSD_INNER), lambda b, p, c: (z_blk(b, p, c), 0)),
            pl.BlockSpec((rows, CONV_CH), lambda b, p, c: (conv_blk(b, p, c), 1)),
            pl.BlockSpec((HALO, CONV_CH), lambda b, p, c: (prev_blk(b, p, c), 1)),
            pl.BlockSpec((HALO, CONV_CH), lambda b, p, c: (next_blk(b, p, c), 1)),
            pl.BlockSpec((rows, LANES), lambda b, p, c: (row_blk(b, p, c), 0)),
            const((8, CONV_CH)), const((1, CONV_CH)), const((1, LANES)), const((1, LANES)),
            const((1, SSD_INNER)), const((1, SSD_INNER)),
        ],
        out_specs=pl.BlockSpec((rows, SSD_INNER), lambda b, p, c: (z_blk(b, p, c), 0)),
        out_shape=jax.ShapeDtypeStruct((t, SSD_INNER), BF16),
        scratch_shapes=[
            pltpu.VMEM((CONV_CH // LANES, rows + 2 * HALO, LANES), F32),
            pltpu.VMEM((seq, CONV_CH), F32),
            pltpu.VMEM((seq, SSD_INNER), F32),
            pltpu.VMEM((SSD_GROUPS, SSD_STATE, SSD_INNER // SSD_GROUPS), F32),
            pltpu.VMEM((2 * LANES, SSD_INNER), BF16),
        ],
        compiler_params=_cparams(("arbitrary", "arbitrary", "arbitrary")),
        name="ssd",
    )(zux, zux, zux, zux, dt, cw, cb, dtb, alog, dskip, nw)


POOL_PAD = 16
POOL_ROWS = 512


def _pool_kernel(u_ref, w_ref, sc_ref, out_ref, pad_ref, *, seq):
    zeros = jnp.zeros((POOL_PAD, POOL_GROUP), F32)
    for g in range(len(POOL_WINDOWS)):
        pad_ref[g, 0:POOL_PAD, :] = zeros
        pad_ref[g, POOL_PAD + seq:, :] = zeros
        pad_ref[g, POOL_PAD:POOL_PAD + seq, :] = u_ref[:, g * POOL_GROUP:(g + 1) * POOL_GROUP]
    for r0 in range(0, seq, POOL_ROWS):
        pos = r0 + lax.broadcasted_iota(jnp.int32, (POOL_ROWS, POOL_GROUP), 0)
        for g, w in enumerate(POOL_WINDOWS):
            cols = slice(g * POOL_GROUP, (g + 1) * POOL_GROUP)
            u = pad_ref[g, POOL_PAD + r0:POOL_PAD + r0 + POOL_ROWS, :]
            s = None
            for j in range(-(w // 2), w // 2):
                term = pad_ref[g, POOL_PAD + r0 + j:POOL_PAD + r0 + j + POOL_ROWS, :]
                s = term if s is None else s + term
            cnt = jnp.minimum(pos + w // 2, seq) - jnp.maximum(pos - w // 2, 0)
            mean = s / cnt.astype(F32)
            y = jnp.dot((mean - u).astype(BF16), w_ref[g], preferred_element_type=F32)
            out_ref[r0:r0 + POOL_ROWS, cols] = (y * sc_ref[:, cols]).astype(BF16)


def _pool(zux, pw, psc, *, batch, seq):
    t = batch * seq
    return pl.pallas_call(
        functools.partial(_pool_kernel, seq=seq),
        scratch_shapes=[pltpu.VMEM((len(POOL_WINDOWS), seq + 2 * POOL_PAD, POOL_GROUP), F32)],
        grid=(batch,),
        in_specs=[
            pl.BlockSpec((seq, POOL_WIDTH), lambda b: (b, SSD_INNER // POOL_WIDTH)),
            pl.BlockSpec((len(POOL_WINDOWS), POOL_GROUP, POOL_GROUP), lambda b: (0, 0, 0)),
            pl.BlockSpec((1, POOL_WIDTH), lambda b: (0, 0)),
        ],
        out_specs=pl.BlockSpec((seq, POOL_WIDTH), lambda b: (b, 0)),
        out_shape=jax.ShapeDtypeStruct((t, POOL_WIDTH), BF16),
        compiler_params=_cparams(("arbitrary",)),
        name="pool",
    )(zux, pw, psc)


def _outproj_kernel(x_ref, a_ref, s_ref, p_ref, wa_ref, ws_ref, wp_ref, o_ref):
    acc = jnp.dot(a_ref[...], wa_ref[...], preferred_element_type=F32)
    acc = acc + jnp.dot(s_ref[...], ws_ref[...], preferred_element_type=F32)
    acc = acc + jnp.dot(p_ref[...], wp_ref[...], preferred_element_type=F32)
    o_ref[...] = x_ref[...] + acc


def _outproj(x2, att, ssd, pool, wa, ws, wp, *, tm, tn):
    t = x2.shape[0]
    return pl.pallas_call(
        _outproj_kernel,
        grid=(D_MODEL // tn, t // tm),
        in_specs=[
            pl.BlockSpec((tm, tn), lambda j, i: (i, j)),
            pl.BlockSpec((tm, ATT_OUT), lambda j, i: (i, 0)),
            pl.BlockSpec((tm, SSD_INNER), lambda j, i: (i, 0)),
            pl.BlockSpec((tm, POOL_WIDTH), lambda j, i: (i, 0)),
            pl.BlockSpec((ATT_OUT, tn), lambda j, i: (0, j)),
            pl.BlockSpec((SSD_INNER, tn), lambda j, i: (0, j)),
            pl.BlockSpec((POOL_WIDTH, tn), lambda j, i: (0, j)),
        ],
        out_specs=pl.BlockSpec((tm, tn), lambda j, i: (i, j)),
        out_shape=jax.ShapeDtypeStruct((t, D_MODEL), F32),
        compiler_params=_cparams(("arbitrary", "arbitrary")),
        name="outproj",
    )(x2, att, ssd, pool, wa, ws, wp)


def _ffn_kernel(x_ref, nw_ref, wg_ref, wu_ref, wd_ref, fw_ref, o_ref, h_ref, *, n_ff, final):
    j = pl.program_id(1)

    @pl.when(j == 0)
    def _():
        x = x_ref[...]
        ms = jnp.mean(x * x, axis=-1, keepdims=True)
        h_ref[...] = (x * lax.rsqrt(ms + RMS_EPS) * nw_ref[...]).astype(BF16)
        o_ref[...] = x

    h = h_ref[...]
    gate = jnp.dot(h, wg_ref[...], preferred_element_type=F32)
    up = jnp.dot(h, wu_ref[...], preferred_element_type=F32)
    act = (_silu(gate) * up).astype(BF16)
    o_ref[...] += jnp.dot(act, wd_ref[...], preferred_element_type=F32)

    if final:
        @pl.when(j == n_ff - 1)
        def _():
            y = o_ref[...]
            ms = jnp.mean(y * y, axis=-1, keepdims=True)
            o_ref[...] = y * lax.rsqrt(ms + RMS_EPS) * fw_ref[...]


def _ffn(x2, nw, wg, wu, wd, fw, *, tm, tf, final):
    t = x2.shape[0]
    n_ff = D_FF // tf
    return pl.pallas_call(
        functools.partial(_ffn_kernel, n_ff=n_ff, final=final),
        grid=(t // tm, n_ff),
        in_specs=[
            pl.BlockSpec((tm, D_MODEL), lambda i, j: (i, 0)),
            pl.BlockSpec((1, D_MODEL), lambda i, j: (0, 0)),
            pl.BlockSpec((D_MODEL, tf), lambda i, j: (0, j)),
            pl.BlockSpec((D_MODEL, tf), lambda i, j: (0, j)),
            pl.BlockSpec((tf, D_MODEL), lambda i, j: (j, 0)),
            pl.BlockSpec((1, D_MODEL), lambda i, j: (0, 0)),
        ],
        out_specs=pl.BlockSpec((tm, D_MODEL), lambda i, j: (i, 0)),
        out_shape=jax.ShapeDtypeStruct((t, D_MODEL), F32),
        scratch_shapes=[pltpu.VMEM((tm, D_MODEL), BF16)],
        compiler_params=_cparams(("arbitrary", "arbitrary")),
        name="ffn_final" if final else "ffn",
    )(x2, nw, wg, wu, wd, fw)


def _pack_in_weights(w_in):
    q0, k0, v0 = 0, ATT_WIDTH, 2 * ATT_WIDTH
    z0 = 3 * ATT_WIDTH
    xbc0 = z0 + SSD_INNER
    dt0 = xbc0 + CONV_CH
    u0 = dt0 + 2 * SSD_HEADS
    tiles = []
    for g in range(N_DIL):
        sl = slice(g * ATT_OUT, (g + 1) * ATT_OUT)
        tiles += [w_in[:, q0:k0][:, sl], w_in[:, k0:v0][:, sl], w_in[:, v0:z0][:, sl]]
    tiles += [w_in[:, z0:xbc0], w_in[:, u0:u0 + POOL_WIDTH], w_in[:, xbc0:dt0]]
    wcat = jnp.concatenate(tiles, axis=1).astype(BF16)
    wdt = jnp.pad(w_in[:, dt0:u0], ((0, 0), (0, LANES - 2 * SSD_HEADS))).astype(BF16)
    return wcat, wdt


def _pad_lanes(v):
    return jnp.pad(v.reshape(1, -1).astype(F32), ((0, 0), (0, LANES - v.size)))


def _layer(x2, slopes, p, fw, *, batch, seq, final):
    wcat, wdt = _pack_in_weights(p["w_in"])
    q0, q1, q2, zux, dt = _inproj(x2, p["norm1_w"].reshape(1, -1), wcat, wdt,
                                  batch=batch, seq=seq, tm=IN_TM)
    ol = [_attention_group(qkv, slopes[g], dil=d) for g, (d, qkv) in enumerate(zip(DILATIONS, (q0, q1, q2)))]
    att = _merge(ol, batch=batch, seq=seq, tm=512)
    cw = jnp.pad(p["conv_w"].astype(F32), ((0, 8 - SSD_CONV), (0, 0)))
    ssd = _ssd(zux, dt, cw, p["conv_b"].reshape(1, -1).astype(F32), _pad_lanes(p["dt_bias"]),
               _pad_lanes(p["a_log"]), jnp.repeat(p["d_skip"].astype(F32), SSD_HEAD_DIM).reshape(1, -1),
               p["ssd_norm_w"].reshape(1, -1).astype(F32), batch=batch, seq=seq)
    pool = _pool(zux, p["pool_w"].astype(BF16), p["pool_scale"].reshape(1, -1).astype(F32),
                 batch=batch, seq=seq)
    wo = p["w_out"].astype(BF16)
    x2 = _outproj(x2, att, ssd, pool, wo[:ATT_OUT], wo[ATT_OUT:ATT_OUT + SSD_INNER],
                  wo[ATT_OUT + SSD_INNER:], tm=1024, tn=OUT_TN)
    return _ffn(x2, p["norm2_w"].reshape(1, -1), p["w_gate"].astype(BF16), p["w_up"].astype(BF16),
                p["w_down"].astype(BF16), fw.reshape(1, -1), tm=FFN_TM, tf=FFN_TF, final=final)


def kernel(x, norm1_w, w_in, conv_w, conv_b, dt_bias, a_log, d_skip, ssd_norm_w, pool_w, pool_scale,
           w_out, norm2_w, w_gate, w_up, w_down, final_norm_w):
    batch, seq, _ = x.shape
    depth = w_in.shape[0]
    k = jnp.arange(1, ATT_HEADS + 1, dtype=F32)
    slopes = (2.0 ** (-8.0 * k / ATT_HEADS)).reshape(N_DIL, ATT_SLOTS)
    x2 = x.reshape(batch * seq, D_MODEL)
    for l in range(depth):
        p = dict(norm1_w=norm1_w[l], w_in=w_in[l], conv_w=conv_w[l], conv_b=conv_b[l], dt_bias=dt_bias[l],
                 a_log=a_log[l], d_skip=d_skip[l], ssd_norm_w=ssd_norm_w[l], pool_w=pool_w[l],
                 pool_scale=pool_scale[l], w_out=w_out[l], norm2_w=norm2_w[l], w_gate=w_gate[l],
                 w_up=w_up[l], w_down=w_down[l])
        x2 = _layer(x2, slopes, p, final_norm_w, batch=batch, seq=seq, final=(l == depth - 1))
    return x2.reshape(batch, seq, D_MODEL)
```

```python
import functools
import math

import jax
import jax.numpy as jnp
from jax import lax
from jax.experimental import pallas as pl
from jax.experimental.pallas import tpu as pltpu

F32 = jnp.float32
BF16 = jnp.bfloat16

D_MODEL = 2048
ATT_HEAD_DIM = 64
ATT_SLOTS = 8
DILATIONS = (1, 4, 16)
ATT_HALF = 64
N_DIL = len(DILATIONS)
ATT_HEADS = ATT_SLOTS * N_DIL
ATT_WIDTH = ATT_HEADS * ATT_HEAD_DIM
ATT_OUT = ATT_SLOTS * ATT_HEAD_DIM
SSD_HEAD_DIM = 64
SSD_INNER = D_MODEL // 2
SSD_HEADS = SSD_INNER // SSD_HEAD_DIM
SSD_GROUPS = 2
SSD_HPG = SSD_HEADS // SSD_GROUPS
SSD_STATE = 128
SSD_CONV = 5
SSD_CHUNK = 128
CONV_CH = SSD_INNER + 2 * SSD_GROUPS * SSD_STATE
POOL_WINDOWS = (2, 4, 8, 16)
POOL_GROUP = 128
POOL_WIDTH = POOL_GROUP * len(POOL_WINDOWS)
MIX_WIDTH = ATT_OUT + SSD_INNER + POOL_WIDTH
D_FF = 5632
RMS_EPS = 1e-6
NEG_INF = -1e30

LANES = 128
IN_TILE = 768
IN_TILES_PER_GROUP = 2
N_IN_TILES = 5 * IN_TILES_PER_GROUP
IN_TM = 1024
VMEM_LIMIT = 56 * 1024 * 1024
OUT_TM = 512
OUT_TN = 2048
FFN_TM = 1024
FFN_TF = 512


def _cparams(sem):
    return pltpu.CompilerParams(dimension_semantics=sem, vmem_limit_bytes=VMEM_LIMIT)


def _silu(x):
    return x * (1.0 / (1.0 + jnp.exp(-x)))


def _softplus(x):
    return jnp.maximum(x, 0.0) + jnp.log(1.0 + jnp.exp(-jnp.abs(x)))


def _inproj_kernel(x_ref, nw_ref, w_ref, wdt_ref, q0_ref, q1_ref, q2_ref, zux_ref, dt_ref,
                   h_ref, stage_ref, *, tm):
    j = pl.program_id(1)
    jg = j // IN_TILES_PER_GROUP

    @pl.when(j == 0)
    def _():
        x = x_ref[...]
        ms = jnp.mean(x * x, axis=-1, keepdims=True)
        h = x * lax.rsqrt(ms + RMS_EPS) * nw_ref[...]
        h_ref[...] = h.astype(BF16)
        dt_ref[...] = jnp.dot(h_ref[...], wdt_ref[...], preferred_element_type=F32)

    def proj():
        return jnp.dot(h_ref[...], w_ref[...], preferred_element_type=F32)

    @pl.when(jg == 0)
    def _():
        q0_ref[0, 0] = proj().astype(BF16)

    for g, (d, out_ref) in enumerate(zip(DILATIONS, (q0_ref, q1_ref, q2_ref))):
        if g == 0:
            continue

        @pl.when(jg == g)
        def _(d=d, out_ref=out_ref):
            res = proj()
            for cb in range(IN_TILE // LANES):
                stage_ref[cb] = res[:, cb * LANES:(cb + 1) * LANES]
            for r in range(d):
                for cb in range(IN_TILE // LANES):
                    out_ref[0, r, :, cb * LANES:(cb + 1) * LANES] = (
                        stage_ref[cb, pl.ds(r, tm // d, stride=d), :].astype(BF16))

    @pl.when(jg >= N_DIL)
    def _():
        zux_ref[...] = proj()


def _inproj(x2, nw, wcat, wdt, *, batch, seq, tm):
    t = x2.shape[0]
    per_b = seq // tm
    grid = (t // tm, N_IN_TILES)
    tpg = IN_TILES_PER_GROUP

    def qspec(g, d):
        return pl.BlockSpec((1, d, tm // d, IN_TILE),
                            lambda i, j: (i // per_b, 0, i % per_b, jnp.clip(j - g * tpg, 0, tpg - 1)))

    out_shape = [jax.ShapeDtypeStruct((batch, d, seq // d, tpg * IN_TILE), BF16) for d in DILATIONS]
    out_shape += [jax.ShapeDtypeStruct((t, 2 * tpg * IN_TILE), F32), jax.ShapeDtypeStruct((t, LANES), F32)]
    return pl.pallas_call(
        functools.partial(_inproj_kernel, tm=tm),
        grid=grid,
        in_specs=[
            pl.BlockSpec((tm, D_MODEL), lambda i, j: (i, 0)),
            pl.BlockSpec((1, D_MODEL), lambda i, j: (0, 0)),
            pl.BlockSpec((D_MODEL, IN_TILE), lambda i, j: (0, j)),
            pl.BlockSpec((D_MODEL, LANES), lambda i, j: (0, 0)),
        ],
        out_specs=[qspec(g, d) for g, d in enumerate(DILATIONS)] + [
            pl.BlockSpec((tm, IN_TILE), lambda i, j: (i, jnp.maximum(j - N_DIL * tpg, 0))),
            pl.BlockSpec((tm, LANES), lambda i, j: (i, 0)),
        ],
        out_shape=out_shape,
        scratch_shapes=[pltpu.VMEM((tm, D_MODEL), BF16), pltpu.VMEM((IN_TILE // LANES, tm, LANES), F32)],
        compiler_params=_cparams(("arbitrary", "arbitrary")),
        name="inproj",
    )(x2, nw, wcat, wdt)


ATT_QB = 128
ATT_KW = 256
ATT_UNROLL = 8


def _attn_kernel(slopes_ref, q_ref, k_ref, v_ref, o_ref, lse_ref, bias_ref, *, seq_len, dil):
    sp = pl.program_id(0)
    is_a = lax.broadcasted_iota(jnp.int32, (ATT_QB, LANES), 1) < ATT_HEAD_DIM
    n_blocks = seq_len // ATT_QB

    @pl.when(pl.program_id(1) == 0)
    def _():
        slope_a = slopes_ref[2 * sp] * float(dil)
        slope_b = slopes_ref[2 * sp + 1] * float(dil)
        rows = lax.broadcasted_iota(jnp.int32, (2 * ATT_QB, ATT_KW), 0)
        cols = lax.broadcasted_iota(jnp.int32, (2 * ATT_QB, ATT_KW), 1)
        rel_base = cols - (rows & (ATT_QB - 1))
        slope = jnp.where(rows < ATT_QB, slope_a, slope_b)
        for t in range(3):
            dist = jnp.abs(rel_base - t * ATT_HALF)
            bias_ref[t] = jnp.where(dist <= ATT_HALF, -slope * dist.astype(F32), NEG_INF)

    def body(it, carry):
        r = lax.shift_right_logical(it, n_blocks.bit_length() - 1)
        i = it & (n_blocks - 1)
        q0 = pl.multiple_of(i * ATT_QB, ATT_QB)
        k0 = pl.multiple_of(jnp.clip(i * ATT_QB - ATT_HALF, 0, seq_len - ATT_KW), ATT_HALF)
        q2 = q_ref[0, r, pl.ds(q0, ATT_QB), :] * jnp.asarray(ATT_HEAD_DIM ** -0.5, BF16)
        zero = jnp.zeros_like(q2)
        qs = jnp.concatenate([jnp.where(is_a, q2, zero), jnp.where(is_a, zero, q2)], axis=0)
        kk = k_ref[0, r, pl.ds(k0, ATT_KW), :]
        vv = v_ref[0, r, pl.ds(k0, ATT_KW), :]
        s = lax.dot_general(qs, kk, (((1,), (1,)), ((), ())), preferred_element_type=F32)
        s = s + bias_ref[lax.shift_right_logical(q0 - k0, ATT_HALF.bit_length() - 1)]
        m = jnp.max(s, axis=-1, keepdims=True)
        p = jnp.exp(s - m)
        l = jnp.sum(p, axis=-1, keepdims=True)
        pv = jnp.dot(p.astype(BF16), vv, preferred_element_type=F32)
        o = pv * (1.0 / l)
        lse = jnp.broadcast_to(m + jnp.log(l), (2 * ATT_QB, LANES))
        o_ref[0, r, pl.ds(q0, ATT_QB), :] = jnp.where(is_a, o[:ATT_QB], o[ATT_QB:]).astype(BF16)
        lse_ref[0, r, pl.ds(q0, ATT_QB), :] = jnp.where(is_a, lse[:ATT_QB], lse[ATT_QB:])
        return carry

    lax.fori_loop(0, dil * n_blocks, body, 0, unroll=ATT_UNROLL)


def _attention_group(qkv, slopes, *, dil):
    batch, d, seq_len, _ = qkv.shape
    n_pairs = ATT_OUT // LANES

    def spec(col0):
        return pl.BlockSpec((1, d, seq_len, LANES), lambda sp, b: (b, 0, 0, col0 + sp))

    shape = (batch, d, seq_len, ATT_OUT)
    return pl.pallas_call(
        functools.partial(_attn_kernel, seq_len=seq_len, dil=dil),
        grid=(n_pairs, batch),
        in_specs=[pl.BlockSpec(memory_space=pltpu.SMEM), spec(0), spec(n_pairs), spec(2 * n_pairs)],
        out_specs=[spec(0), spec(0)],
        out_shape=[jax.ShapeDtypeStruct(shape, BF16), jax.ShapeDtypeStruct(shape, F32)],
        scratch_shapes=[pltpu.VMEM((3, 2 * ATT_QB, ATT_KW), F32)],
        compiler_params=_cparams(("arbitrary", "arbitrary")),
        name=f"attn_d{dil}",
    )(slopes, qkv, qkv, qkv)


def _merge_kernel(o0_ref, l0_ref, o1_ref, l1_ref, o2_ref, l2_ref, att_ref, so_ref, sl_ref, *, tm):
    outs = [o0_ref[0, 0].astype(F32)]
    lses = [l0_ref[0, 0]]
    for d, o_ref, l_ref in ((DILATIONS[1], o1_ref, l1_ref), (DILATIONS[2], o2_ref, l2_ref)):
        for r in range(d):
            for cb in range(ATT_OUT // LANES):
                cols = slice(cb * LANES, (cb + 1) * LANES)
                so_ref[cb, pl.ds(r, tm // d, stride=d), :] = o_ref[0, r, :, cols].astype(F32)
                sl_ref[cb, pl.ds(r, tm // d, stride=d), :] = l_ref[0, r, :, cols]
        outs.append(jnp.concatenate([so_ref[cb] for cb in range(ATT_OUT // LANES)], axis=1))
        lses.append(jnp.concatenate([sl_ref[cb] for cb in range(ATT_OUT // LANES)], axis=1))
    mx = jnp.maximum(jnp.maximum(lses[0], lses[1]), lses[2])
    ws = [jnp.exp(l - mx) for l in lses]
    num = ws[0] * outs[0] + ws[1] * outs[1] + ws[2] * outs[2]
    att_ref[...] = (num / (ws[0] + ws[1] + ws[2])).astype(BF16)


def _merge(ol, *, batch, seq, tm):
    t = batch * seq
    per_b = seq // tm
    in_specs = []
    args = []
    for d, (o, l) in zip(DILATIONS, ol):
        spec = pl.BlockSpec((1, d, tm // d, ATT_OUT), lambda i: (i // per_b, 0, i % per_b, 0))
        in_specs += [spec, spec]
        args += [o, l]
    return pl.pallas_call(
        functools.partial(_merge_kernel, tm=tm),
        grid=(t // tm,),
        in_specs=in_specs,
        out_specs=pl.BlockSpec((tm, ATT_OUT), lambda i: (i, 0)),
        out_shape=jax.ShapeDtypeStruct((t, ATT_OUT), BF16),
        scratch_shapes=[pltpu.VMEM((ATT_OUT // LANES, tm, LANES), F32)] * 2,
        compiler_params=_cparams(("arbitrary",)),
        name="merge",
    )(*args)


HALO = 8
SSD_SUB = 2
SSD_ROWS = SSD_SUB * SSD_CHUNK


def _expand_rows(fac, col0, is_a):
    parts = []
    for j in range(SSD_HPG // 2):
        a = fac[:, col0 + 2 * j:col0 + 2 * j + 1]
        b = fac[:, col0 + 2 * j + 1:col0 + 2 * j + 2]
        parts.append(jnp.where(is_a, a, b))
    return jnp.concatenate(parts, axis=1)


def _split2(v):
    hi = v.astype(BF16)
    lo = (v - hi.astype(F32)).astype(BF16)
    return jnp.concatenate([hi, lo], axis=1)


def _split3(v):
    hi = v.astype(BF16)
    r1 = v - hi.astype(F32)
    mid = r1.astype(BF16)
    lo = (r1 - mid.astype(F32)).astype(BF16)
    return jnp.concatenate([hi, mid, lo], axis=1)


def _ssd_kernel(z_ref, xbc_ref, prev_ref, next_ref, dt_ref, cw_ref, cb_ref, dtb_ref, alog_ref,
                dskip_ref, nw_ref, out_ref, ext_ref, xc_ref, yf_ref, st_ref, sel_ref, *, n_blocks):
    q = SSD_CHUNK
    p = pl.program_id(1)
    c = pl.program_id(2)
    fwd = p == 0
    cc = jnp.where(fwd, c, n_blocks - 1 - c)
    row0 = pl.multiple_of(cc * SSD_ROWS, SSD_ROWS)

    @pl.when(jnp.logical_and(fwd, c == 0))
    def _():
        yf_ref[...] = jnp.zeros_like(yf_ref)

    @pl.when(c == 0)
    def _():
        st_ref[...] = jnp.zeros_like(st_ref)
        src = lax.broadcasted_iota(jnp.int32, (2 * LANES, SSD_INNER), 0) & (LANES - 1)
        dst = lax.shift_right_logical(lax.broadcasted_iota(jnp.int32, (2 * LANES, SSD_INNER), 1),
                                      SSD_HEAD_DIM.bit_length() - 1)
        sel_ref[...] = jnp.where(src == dst, 1.0, 0.0).astype(BF16)

    @pl.when(fwd)
    def _():
        for cb in range(CONV_CH // LANES):
            cols = slice(cb * LANES, (cb + 1) * LANES)
            ext_ref[cb, 0:HALO, :] = jnp.where(cc == 0, 0.0, prev_ref[:, cols])
            ext_ref[cb, HALO:HALO + SSD_ROWS, :] = xbc_ref[:, cols]
            ext_ref[cb, HALO + SSD_ROWS:, :] = jnp.where(cc == n_blocks - 1, 0.0, next_ref[:, cols])
            acc = jnp.broadcast_to(cb_ref[:, cols], (SSD_ROWS, LANES))
            for k in range(SSD_CONV):
                acc = acc + cw_ref[k:k + 1, cols] * ext_ref[cb, pl.ds(HALO - SSD_CONV // 2 + k, SSD_ROWS), :]
            xc_ref[pl.ds(row0, SSD_ROWS), cols] = _silu(acc)

    gn = SSD_GROUPS * SSD_STATE
    half = SSD_INNER // SSD_GROUPS
    ri = lax.broadcasted_iota(jnp.int32, (q, q), 0)
    ci = lax.broadcasted_iota(jnp.int32, (q, q), 1)
    before = (ci - ri) * jnp.where(fwd, 1, -1) <= 0
    cum_lhs = jnp.concatenate([before.astype(BF16), jnp.ones((q, q), BF16)], axis=0)
    is_a = lax.broadcasted_iota(jnp.int32, (q, LANES), 1) < SSD_HEAD_DIM
    neg_a = -jnp.exp(alog_ref[...])

    for k in range(SSD_SUB):
        off = pl.multiple_of(jnp.where(fwd, k, SSD_SUB - 1 - k) * q, q)
        rows = pl.ds(row0 + off, q)
        xs = xc_ref[rows, 0:SSD_INNER]

        dt_all = _softplus(dt_ref[pl.ds(off, q), :] + dtb_ref[...])
        dta_all = dt_all * neg_a
        dt_d = jnp.where(fwd, dt_all, pltpu.roll(dt_all, LANES - SSD_HEADS, 1))
        dta_d = jnp.where(fwd, dta_all, pltpu.roll(dta_all, LANES - SSD_HEADS, 1))
        sums = jnp.dot(cum_lhs, _split3(dta_d), preferred_element_type=F32)
        sums = sums[:, 0:LANES] + sums[:, LANES:2 * LANES] + sums[:, 2 * LANES:]
        cum = sums[:q]
        total = sums[q:]
        wide = jnp.dot(jnp.concatenate([_split2(jnp.exp(cum)), _split2(jnp.exp(total - cum) * dt_d)], axis=0),
                       sel_ref[...], preferred_element_type=F32)
        e_cum_w = wide[:q]
        d_state_w = wide[q:]
        d_chunk = jnp.exp(total[0:8])
        log_dt_t = jnp.log(dt_d).T
        shifted_t = cum.T - log_dt_t

        ys = []
        for g in range(SSD_GROUPS):
            col0 = g * SSD_HPG
            gcols = slice(g * half, (g + 1) * half)
            xg = xs[:, gcols]
            bg = xc_ref[rows, SSD_INNER + g * SSD_STATE:SSD_INNER + (g + 1) * SSD_STATE]
            cg = xc_ref[rows, SSD_INNER + gn + g * SSD_STATE:SSD_INNER + gn + (g + 1) * SSD_STATE]
            xg16 = xg.astype(BF16)
            bg16 = bg.astype(BF16)
            cg16 = cg.astype(BF16)
            cbm = lax.dot_general(cg16, bg16, (((1,), (1,)), ((), ())), preferred_element_type=F32)
            cbm = jnp.where(before, cbm, 0.0)
            y_parts = []
            for j in range(SSD_HPG // 2):
                ms = []
                for h in (2 * j, 2 * j + 1):
                    col = col0 + h
                    diff = cum[:, col:col + 1] - shifted_t[col:col + 1, :]
                    ms.append((cbm * jnp.exp(jnp.minimum(diff, log_dt_t[col:col + 1, :]))).astype(BF16))
                xp = xg16[:, j * LANES:(j + 1) * LANES]
                zero = jnp.zeros_like(xp)
                xdiag = jnp.concatenate([jnp.where(is_a, xp, zero), jnp.where(is_a, zero, xp)], axis=0)
                y_parts.append(jnp.dot(jnp.concatenate(ms, axis=1), xdiag, preferred_element_type=F32))
            y_diag = jnp.concatenate(y_parts, axis=1)
            st = st_ref[g]
            y_off = jnp.dot(cg16, st.astype(BF16), preferred_element_type=F32) * e_cum_w[:, gcols]
            xd = (xg * d_state_w[:, gcols]).astype(BF16)
            new = jnp.dot(bg.T.astype(BF16), xd, preferred_element_type=F32)
            keep = _expand_rows(d_chunk, col0, is_a[0:8])
            st_ref[g] = (st.reshape(SSD_STATE // 8, 8, -1) * keep[None]).reshape(st.shape) + new
            ys.append(y_diag + y_off)
        y = jnp.concatenate(ys, axis=1)
        yf_ref[rows, :] += y

    @pl.when(jnp.logical_not(fwd))
    def _():
        rows = pl.ds(row0, SSD_ROWS)
        yt = (yf_ref[rows, :] + xc_ref[rows, 0:SSD_INNER] * dskip_ref[...]) * _silu(z_ref[...])
        for g in range(SSD_GROUPS):
            v = yt[:, g * half:(g + 1) * half]
            ms = jnp.mean(v * v, axis=-1, keepdims=True)
            out_ref[:, g * half:(g + 1) * half] = (
                v * lax.rsqrt(ms + RMS_EPS) * nw_ref[:, g * half:(g + 1) * half]).astype(BF16)


def _ssd(zux, dt, cw, cb, dtb, alog, dskip, nw, *, batch, seq):
    t = batch * seq
    rows = SSD_ROWS
    n_blocks = seq // rows
    hb = rows // HALO

    def row_blk(b, p, c):
        return b * n_blocks + jnp.where(p == 0, c, n_blocks - 1 - c)

    def conv_blk(b, p, c):
        return b * n_blocks + jnp.where(p == 0, c, n_blocks - 1)

    def prev_blk(b, p, c):
        return jnp.maximum(conv_blk(b, p, c) * hb - 1, b * n_blocks * hb)

    def next_blk(b, p, c):
        return jnp.minimum((conv_blk(b, p, c) + 1) * hb, (b + 1) * n_blocks * hb - 1)

    def z_blk(b, p, c):
        return jnp.where(p == 0, b * n_blocks + n_blocks - 1, row_blk(b, p, c))

    def const(shape):
        return pl.BlockSpec(shape, lambda b, p, c: (0, 0))

    return pl.pallas_call(
        functools.partial(_ssd_kernel, n_blocks=n_blocks),
        grid=(batch, 2, n_blocks),
        in_specs=[
            pl.BlockSpec((rows, SSD_INNER), lambda b, p, c: (z_blk(b, p, c), 0)),
            pl.BlockSpec((rows, CONV_CH), lambda b, p, c: (conv_blk(b, p, c), 1)),
            pl.BlockSpec((HALO, CONV_CH), lambda b, p, c: (prev_blk(b, p, c), 1)),
            pl.BlockSpec((HALO, CONV_CH), lambda b, p, c: (next_blk(b, p, c), 1)),
            pl.BlockSpec((rows, LANES), lambda b, p, c: (row_blk(b, p, c), 0)),
            const((8, CONV_CH)), const((1, CONV_CH)), const((1, LANES)), const((1, LANES)),
            const((1, SSD_INNER)), const((1, SSD_INNER)),
        ],
        out_specs=pl.BlockSpec((rows, SSD_INNER), lambda b, p, c: (z_blk(b, p, c), 0)),
        out_shape=jax.ShapeDtypeStruct((t, SSD_INNER), BF16),
        scratch_shapes=[
            pltpu.VMEM((CONV_CH // LANES, rows + 2 * HALO, LANES), F32),
            pltpu.VMEM((seq, CONV_CH), F32),
            pltpu.VMEM((seq, SSD_INNER), F32),
            pltpu.VMEM((SSD_GROUPS, SSD_STATE, SSD_INNER // SSD_GROUPS), F32),
            pltpu.VMEM((2 * LANES, SSD_INNER), BF16),
        ],
        compiler_params=_cparams(("arbitrary", "arbitrary", "arbitrary")),
        name="ssd",
    )(zux, zux, zux, zux, dt, cw, cb, dtb, alog, dskip, nw)


POOL_PAD = 16
POOL_ROWS = 512


def _pool_kernel(u_ref, w_ref, sc_ref, out_ref, pad_ref, *, seq):
    zeros = jnp.zeros((POOL_PAD, POOL_GROUP), F32)
    for g in range(len(POOL_WINDOWS)):
        pad_ref[g, 0:POOL_PAD, :] = zeros
        pad_ref[g, POOL_PAD + seq:, :] = zeros
        pad_ref[g, POOL_PAD:POOL_PAD + seq, :] = u_ref[:, g * POOL_GROUP:(g + 1) * POOL_GROUP]
    for r0 in range(0, seq, POOL_ROWS):
        pos = r0 + lax.broadcasted_iota(jnp.int32, (POOL_ROWS, POOL_GROUP), 0)
        for g, w in enumerate(POOL_WINDOWS):
            cols = slice(g * POOL_GROUP, (g + 1) * POOL_GROUP)
            u = pad_ref[g, POOL_PAD + r0:POOL_PAD + r0 + POOL_ROWS, :]
            s = None
            for j in range(-(w // 2), w // 2):
                term = pad_ref[g, POOL_PAD + r0 + j:POOL_PAD + r0 + j + POOL_ROWS, :]
                s = term if s is None else s + term
            cnt = jnp.minimum(pos + w // 2, seq) - jnp.maximum(pos - w // 2, 0)
            mean = s / cnt.astype(F32)
            y = jnp.dot((mean - u).astype(BF16), w_ref[g], preferred_element_type=F32)
            out_ref[r0:r0 + POOL_ROWS, cols] = (y * sc_ref[:, cols]).astype(BF16)


def _pool(zux, pw, psc, *, batch, seq):
    t = batch * seq
    return pl.pallas_call(
        functools.partial(_pool_kernel, seq=seq),
        scratch_shapes=[pltpu.VMEM((len(POOL_WINDOWS), seq + 2 * POOL_PAD, POOL_GROUP), F32)],
        grid=(batch,),
        in_specs=[
            pl.BlockSpec((seq, POOL_WIDTH), lambda b: (b, SSD_INNER // POOL_WIDTH)),
            pl.BlockSpec((len(POOL_WINDOWS), POOL_GROUP, POOL_GROUP), lambda b: (0, 0, 0)),
            pl.BlockSpec((1, POOL_WIDTH), lambda b: (0, 0)),
        ],
        out_specs=pl.BlockSpec((seq, POOL_WIDTH), lambda b: (b, 0)),
        out_shape=jax.ShapeDtypeStruct((t, POOL_WIDTH), BF16),
        compiler_params=_cparams(("arbitrary",)),
        name="pool",
    )(zux, pw, psc)


def _outproj_kernel(x_ref, a_ref, s_ref, p_ref, wa_ref, ws_ref, wp_ref, o_ref):
    acc = jnp.dot(a_ref[...], wa_ref[...], preferred_element_type=F32)
    acc = acc + jnp.dot(s_ref[...], ws_ref[...], preferred_element_type=F32)
    acc = acc + jnp.dot(p_ref[...], wp_ref[...], preferred_element_type=F32)
    o_ref[...] = x_ref[...] + acc


def _outproj(x2, att, ssd, pool, wa, ws, wp, *, tm, tn):
    t = x2.shape[0]
    return pl.pallas_call(
        _outproj_kernel,
        grid=(D_MODEL // tn, t // tm),
        in_specs=[
            pl.BlockSpec((tm, tn), lambda j, i: (i, j)),
            pl.BlockSpec((tm, ATT_OUT), lambda j, i: (i, 0)),
            pl.BlockSpec((tm, SSD_INNER), lambda j, i: (i, 0)),
            pl.BlockSpec((tm, POOL_WIDTH), lambda j, i: (i, 0)),
            pl.BlockSpec((ATT_OUT, tn), lambda j, i: (0, j)),
            pl.BlockSpec((SSD_INNER, tn), lambda j, i: (0, j)),
            pl.BlockSpec((POOL_WIDTH, tn), lambda j, i: (0, j)),
        ],
        out_specs=pl.BlockSpec((tm, tn), lambda j, i: (i, j)),
        out_shape=jax.ShapeDtypeStruct((t, D_MODEL), F32),
        compiler_params=_cparams(("arbitrary", "arbitrary")),
        name="outproj",
    )(x2, att, ssd, pool, wa, ws, wp)


def _ffn_kernel(x_ref, nw_ref, wg_ref, wu_ref, wd_ref, fw_ref, o_ref, h_ref, *, n_ff, final):
    j = pl.program_id(1)

    @pl.when(j == 0)
    def _():
        x = x_ref[...]
        ms = jnp.mean(x * x, axis=-1, keepdims=True)
        h_ref[...] = (x * lax.rsqrt(ms + RMS_EPS) * nw_ref[...]).astype(BF16)
        o_ref[...] = x

    h = h_ref[...]
    gate = jnp.dot(h, wg_ref[...], preferred_element_type=F32)
    up = jnp.dot(h, wu_ref[...], preferred_element_type=F32)
    act = (_silu(gate) * up).astype(BF16)
    o_ref[...] += jnp.dot(act, wd_ref[...], preferred_element_type=F32)

    if final:
        @pl.when(j == n_ff - 1)
        def _():
            y = o_ref[...]
            ms = jnp.mean(y * y, axis=-1, keepdims=True)
            o_ref[...] = y * lax.rsqrt(ms + RMS_EPS) * fw_ref[...]


def _ffn(x2, nw, wg, wu, wd, fw, *, tm, tf, final):
    t = x2.shape[0]
    n_ff = D_FF // tf
    return pl.pallas_call(
        functools.partial(_ffn_kernel, n_ff=n_ff, final=final),
        grid=(t // tm, n_ff),
        in_specs=[
            pl.BlockSpec((tm, D_MODEL), lambda i, j: (i, 0)),
            pl.BlockSpec((1, D_MODEL), lambda i, j: (0, 0)),
            pl.BlockSpec((D_MODEL, tf), lambda i, j: (0, j)),
            pl.BlockSpec((D_MODEL, tf), lambda i, j: (0, j)),
            pl.BlockSpec((tf, D_MODEL), lambda i, j: (j, 0)),
            pl.BlockSpec((1, D_MODEL), lambda i, j: (0, 0)),
        ],
        out_specs=pl.BlockSpec((tm, D_MODEL), lambda i, j: (i, 0)),
        out_shape=jax.ShapeDtypeStruct((t, D_MODEL), F32),
        scratch_shapes=[pltpu.VMEM((tm, D_MODEL), BF16)],
        compiler_params=_cparams(("arbitrary", "arbitrary")),
        name="ffn_final" if final else "ffn",
    )(x2, nw, wg, wu, wd, fw)


PACK_W = 512
PACK_U_TILE = 11


def _pack_kernel(w_ref, u_ref, o_ref):
    j = pl.program_id(0)

    @pl.when(j != PACK_U_TILE)
    def _():
        o_ref[...] = w_ref[0].astype(BF16)

    @pl.when(j == PACK_U_TILE)
    def _():
        o_ref[...] = u_ref[...].astype(BF16)


def _pack_in_weights(w_in, layer):
    z0 = 3 * ATT_WIDTH
    dt0 = z0 + SSD_INNER + CONV_CH
    u0 = dt0 + 2 * SSD_HEADS
    n_qkv = z0 // PACK_W

    def src_tile(j):
        qkv = (j % N_DIL) * 3 + j // N_DIL
        return jnp.where(j < n_qkv, qkv, jnp.where(j < PACK_U_TILE, j, j - 1))

    wcat = pl.pallas_call(
        _pack_kernel,
        grid=(N_IN_TILES * IN_TILE // PACK_W,),
        in_specs=[
            pl.BlockSpec((1, D_MODEL, PACK_W), lambda j: (layer, 0, src_tile(j))),
            pl.BlockSpec((D_MODEL, PACK_W), lambda j: (0, 0)),
        ],
        out_specs=pl.BlockSpec((D_MODEL, PACK_W), lambda j: (0, j)),
        out_shape=jax.ShapeDtypeStruct((D_MODEL, N_IN_TILES * IN_TILE), BF16),
        compiler_params=_cparams(("arbitrary",)),
        name="pack_w_in",
    )(w_in, w_in[layer, :, u0:u0 + POOL_WIDTH])
    wdt = jnp.pad(w_in[layer, :, dt0:u0], ((0, 0), (0, LANES - 2 * SSD_HEADS))).astype(BF16)
    return wcat, wdt


def _pad_lanes(v):
    return jnp.pad(v.reshape(1, -1).astype(F32), ((0, 0), (0, LANES - v.size)))


def _layer(x2, slopes, p, fw, *, batch, seq, final):
    wcat, wdt = _pack_in_weights(p["w_in"], p["layer"])
    q0, q1, q2, zux, dt = _inproj(x2, p["norm1_w"].reshape(1, -1), wcat, wdt,
                                  batch=batch, seq=seq, tm=IN_TM)
    ol = [_attention_group(qkv, slopes[g], dil=d) for g, (d, qkv) in enumerate(zip(DILATIONS, (q0, q1, q2)))]
    att = _merge(ol, batch=batch, seq=seq, tm=512)
    cw = jnp.pad(p["conv_w"].astype(F32), ((0, 8 - SSD_CONV), (0, 0)))
    ssd = _ssd(zux, dt, cw, p["conv_b"].reshape(1, -1).astype(F32), _pad_lanes(p["dt_bias"]),
               _pad_lanes(p["a_log"]), jnp.repeat(p["d_skip"].astype(F32), SSD_HEAD_DIM).reshape(1, -1),
               p["ssd_norm_w"].reshape(1, -1).astype(F32), batch=batch, seq=seq)
    pool = _pool(zux, p["pool_w"].astype(BF16), p["pool_scale"].reshape(1, -1).astype(F32),
                 batch=batch, seq=seq)
    wo = p["w_out"].astype(BF16)
    x2 = _outproj(x2, att, ssd, pool, wo[:ATT_OUT], wo[ATT_OUT:ATT_OUT + SSD_INNER],
                  wo[ATT_OUT + SSD_INNER:], tm=OUT_TM, tn=OUT_TN)
    return _ffn(x2, p["norm2_w"].reshape(1, -1), p["w_gate"].astype(BF16), p["w_up"].astype(BF16),
                p["w_down"].astype(BF16), fw.reshape(1, -1), tm=FFN_TM, tf=FFN_TF, final=final)


def kernel(x, norm1_w, w_in, conv_w, conv_b, dt_bias, a_log, d_skip, ssd_norm_w, pool_w, pool_scale,
           w_out, norm2_w, w_gate, w_up, w_down, final_norm_w):
    batch, seq, _ = x.shape
    depth = w_in.shape[0]
    k = jnp.arange(1, ATT_HEADS + 1, dtype=F32)
    slopes = (2.0 ** (-8.0 * k / ATT_HEADS)).reshape(N_DIL, ATT_SLOTS)
    x2 = x.reshape(batch * seq, D_MODEL)
    for l in range(depth):
        p = dict(norm1_w=norm1_w[l], w_in=w_in, layer=l, conv_w=conv_w[l], conv_b=conv_b[l], dt_bias=dt_bias[l],
                 a_log=a_log[l], d_skip=d_skip[l], ssd_norm_w=ssd_norm_w[l], pool_w=pool_w[l],
                 pool_scale=pool_scale[l], w_out=w_out[l], norm2_w=norm2_w[l], w_gate=w_gate[l],
                 w_up=w_up[l], w_down=w_down[l])
        x2 = _layer(x2, slopes, p, final_norm_w, batch=batch, seq=seq, final=(l == depth - 1))
    return x2.reshape(batch, seq, D_MODEL)
```

```python
import functools
import math

import jax
import jax.numpy as jnp
from jax import lax
from jax.experimental import pallas as pl
from jax.experimental.pallas import tpu as pltpu

F32 = jnp.float32
BF16 = jnp.bfloat16

D_MODEL = 2048
ATT_HEAD_DIM = 64
ATT_SLOTS = 8
DILATIONS = (1, 4, 16)
ATT_HALF = 64
N_DIL = len(DILATIONS)
ATT_HEADS = ATT_SLOTS * N_DIL
ATT_WIDTH = ATT_HEADS * ATT_HEAD_DIM
ATT_OUT = ATT_SLOTS * ATT_HEAD_DIM
SSD_HEAD_DIM = 64
SSD_INNER = D_MODEL // 2
SSD_HEADS = SSD_INNER // SSD_HEAD_DIM
SSD_GROUPS = 2
SSD_HPG = SSD_HEADS // SSD_GROUPS
SSD_STATE = 128
SSD_CONV = 5
SSD_CHUNK = 128
CONV_CH = SSD_INNER + 2 * SSD_GROUPS * SSD_STATE
POOL_WINDOWS = (2, 4, 8, 16)
POOL_GROUP = 128
POOL_WIDTH = POOL_GROUP * len(POOL_WINDOWS)
MIX_WIDTH = ATT_OUT + SSD_INNER + POOL_WIDTH
D_FF = 5632
RMS_EPS = 1e-6
NEG_INF = -1e30

LANES = 128
IN_TILE = 768
IN_TILES_PER_GROUP = 2
N_IN_TILES = 5 * IN_TILES_PER_GROUP
IN_TM = 1024
VMEM_LIMIT = 56 * 1024 * 1024
OUT_TM = 512
OUT_TN = 2048
FFN_TM = 1024
FFN_TF = 512


def _cparams(sem):
    return pltpu.CompilerParams(dimension_semantics=sem, vmem_limit_bytes=VMEM_LIMIT)


def _silu(x):
    return x * (1.0 / (1.0 + jnp.exp(-x)))


def _softplus(x):
    return jnp.maximum(x, 0.0) + jnp.log(1.0 + jnp.exp(-jnp.abs(x)))


def _inproj_kernel(x_ref, nw_ref, w_ref, wdt_ref, q0_ref, q1_ref, q2_ref, zux_ref, dt_ref,
                   h_ref, stage_ref, *, tm):
    j = pl.program_id(1)
    jg = j // IN_TILES_PER_GROUP

    @pl.when(j == 0)
    def _():
        x = x_ref[...]
        ms = jnp.mean(x * x, axis=-1, keepdims=True)
        h = x * lax.rsqrt(ms + RMS_EPS) * nw_ref[...]
        h_ref[...] = h.astype(BF16)
        dt_ref[...] = jnp.dot(h_ref[...], wdt_ref[...], preferred_element_type=F32)

    def proj():
        return jnp.dot(h_ref[...], w_ref[...], preferred_element_type=F32)

    @pl.when(jg == 0)
    def _():
        q0_ref[0, 0] = proj().astype(BF16)

    for g, (d, out_ref) in enumerate(zip(DILATIONS, (q0_ref, q1_ref, q2_ref))):
        if g == 0:
            continue

        @pl.when(jg == g)
        def _(d=d, out_ref=out_ref):
            res = proj()
            for cb in range(IN_TILE // LANES):
                stage_ref[cb] = res[:, cb * LANES:(cb + 1) * LANES]
            for r in range(d):
                for cb in range(IN_TILE // LANES):
                    out_ref[0, r, :, cb * LANES:(cb + 1) * LANES] = (
                        stage_ref[cb, pl.ds(r, tm // d, stride=d), :].astype(BF16))

    @pl.when(jg >= N_DIL)
    def _():
        zux_ref[...] = proj()


def _inproj(x2, nw, wcat, wdt, *, batch, seq, tm):
    t = x2.shape[0]
    per_b = seq // tm
    grid = (t // tm, N_IN_TILES)
    tpg = IN_TILES_PER_GROUP

    def qspec(g, d):
        return pl.BlockSpec((1, d, tm // d, IN_TILE),
                            lambda i, j: (i // per_b, 0, i % per_b, jnp.clip(j - g * tpg, 0, tpg - 1)))

    out_shape = [jax.ShapeDtypeStruct((batch, d, seq // d, tpg * IN_TILE), BF16) for d in DILATIONS]
    out_shape += [jax.ShapeDtypeStruct((t, 2 * tpg * IN_TILE), F32), jax.ShapeDtypeStruct((t, LANES), F32)]
    return pl.pallas_call(
        functools.partial(_inproj_kernel, tm=tm),
        grid=grid,
        in_specs=[
            pl.BlockSpec((tm, D_MODEL), lambda i, j: (i, 0)),
            pl.BlockSpec((1, D_MODEL), lambda i, j: (0, 0)),
            pl.BlockSpec((D_MODEL, IN_TILE), lambda i, j: (0, j)),
            pl.BlockSpec((D_MODEL, LANES), lambda i, j: (0, 0)),
        ],
        out_specs=[qspec(g, d) for g, d in enumerate(DILATIONS)] + [
            pl.BlockSpec((tm, IN_TILE), lambda i, j: (i, jnp.maximum(j - N_DIL * tpg, 0))),
            pl.BlockSpec((tm, LANES), lambda i, j: (i, 0)),
        ],
        out_shape=out_shape,
        scratch_shapes=[pltpu.VMEM((tm, D_MODEL), BF16), pltpu.VMEM((IN_TILE // LANES, tm, LANES), F32)],
        compiler_params=_cparams(("arbitrary", "arbitrary")),
        name="inproj",
    )(x2, nw, wcat, wdt)


ATT_QB = 128
ATT_KW = 256
ATT_UNROLL = 8


def _attn_kernel(slopes_ref, q_ref, k_ref, v_ref, o_ref, lse_ref, bias_ref, *, seq_len, dil):
    sp = pl.program_id(0)
    is_a = lax.broadcasted_iota(jnp.int32, (ATT_QB, LANES), 1) < ATT_HEAD_DIM
    n_blocks = seq_len // ATT_QB

    @pl.when(pl.program_id(1) == 0)
    def _():
        slope_a = slopes_ref[2 * sp] * float(dil)
        slope_b = slopes_ref[2 * sp + 1] * float(dil)
        rows = lax.broadcasted_iota(jnp.int32, (2 * ATT_QB, ATT_KW), 0)
        cols = lax.broadcasted_iota(jnp.int32, (2 * ATT_QB, ATT_KW), 1)
        rel_base = cols - (rows & (ATT_QB - 1))
        slope = jnp.where(rows < ATT_QB, slope_a, slope_b)
        for t in range(3):
            dist = jnp.abs(rel_base - t * ATT_HALF)
            bias_ref[t] = jnp.where(dist <= ATT_HALF, -slope * dist.astype(F32), NEG_INF)

    def body(it, carry):
        r = lax.shift_right_logical(it, n_blocks.bit_length() - 1)
        i = it & (n_blocks - 1)
        q0 = pl.multiple_of(i * ATT_QB, ATT_QB)
        k0 = pl.multiple_of(jnp.clip(i * ATT_QB - ATT_HALF, 0, seq_len - ATT_KW), ATT_HALF)
        q2 = q_ref[0, r, pl.ds(q0, ATT_QB), :] * jnp.asarray(ATT_HEAD_DIM ** -0.5, BF16)
        zero = jnp.zeros_like(q2)
        qs = jnp.concatenate([jnp.where(is_a, q2, zero), jnp.where(is_a, zero, q2)], axis=0)
        kk = k_ref[0, r, pl.ds(k0, ATT_KW), :]
        vv = v_ref[0, r, pl.ds(k0, ATT_KW), :]
        s = lax.dot_general(qs, kk, (((1,), (1,)), ((), ())), preferred_element_type=F32)
        s = s + bias_ref[lax.shift_right_logical(q0 - k0, ATT_HALF.bit_length() - 1)]
        m = jnp.max(s, axis=-1, keepdims=True)
        p = jnp.exp(s - m)
        l = jnp.sum(p, axis=-1, keepdims=True)
        pv = jnp.dot(p.astype(BF16), vv, preferred_element_type=F32)
        o = pv * (1.0 / l)
        lse = jnp.broadcast_to(m + jnp.log(l), (2 * ATT_QB, LANES))
        o_ref[0, r, pl.ds(q0, ATT_QB), :] = jnp.where(is_a, o[:ATT_QB], o[ATT_QB:]).astype(BF16)
        lse_ref[0, r, pl.ds(q0, ATT_QB), :] = jnp.where(is_a, lse[:ATT_QB], lse[ATT_QB:])
        return carry

    lax.fori_loop(0, dil * n_blocks, body, 0, unroll=ATT_UNROLL)


def _attention_group(qkv, slopes, *, dil):
    batch, d, seq_len, _ = qkv.shape
    n_pairs = ATT_OUT // LANES

    def spec(col0):
        return pl.BlockSpec((1, d, seq_len, LANES), lambda sp, b: (b, 0, 0, col0 + sp))

    shape = (batch, d, seq_len, ATT_OUT)
    return pl.pallas_call(
        functools.partial(_attn_kernel, seq_len=seq_len, dil=dil),
        grid=(n_pairs, batch),
        in_specs=[pl.BlockSpec(memory_space=pltpu.SMEM), spec(0), spec(n_pairs), spec(2 * n_pairs)],
        out_specs=[spec(0), spec(0)],
        out_shape=[jax.ShapeDtypeStruct(shape, BF16), jax.ShapeDtypeStruct(shape, F32)],
        scratch_shapes=[pltpu.VMEM((3, 2 * ATT_QB, ATT_KW), F32)],
        compiler_params=_cparams(("arbitrary", "arbitrary")),
        name=f"attn_d{dil}",
    )(slopes, qkv, qkv, qkv)


def _merge_kernel(o0_ref, l0_ref, o1_ref, l1_ref, o2_ref, l2_ref, att_ref, so_ref, sl_ref, *, tm):
    outs = [o0_ref[0, 0].astype(F32)]
    lses = [l0_ref[0, 0]]
    for d, o_ref, l_ref in ((DILATIONS[1], o1_ref, l1_ref), (DILATIONS[2], o2_ref, l2_ref)):
        for r in range(d):
            for cb in range(ATT_OUT // LANES):
                cols = slice(cb * LANES, (cb + 1) * LANES)
                so_ref[cb, pl.ds(r, tm // d, stride=d), :] = o_ref[0, r, :, cols].astype(F32)
                sl_ref[cb, pl.ds(r, tm // d, stride=d), :] = l_ref[0, r, :, cols]
        outs.append(jnp.concatenate([so_ref[cb] for cb in range(ATT_OUT // LANES)], axis=1))
        lses.append(jnp.concatenate([sl_ref[cb] for cb in range(ATT_OUT // LANES)], axis=1))
    mx = jnp.maximum(jnp.maximum(lses[0], lses[1]), lses[2])
    ws = [jnp.exp(l - mx) for l in lses]
    num = ws[0] * outs[0] + ws[1] * outs[1] + ws[2] * outs[2]
    att_ref[...] = (num / (ws[0] + ws[1] + ws[2])).astype(BF16)


def _merge(ol, *, batch, seq, tm):
    t = batch * seq
    per_b = seq // tm
    in_specs = []
    args = []
    for d, (o, l) in zip(DILATIONS, ol):
        spec = pl.BlockSpec((1, d, tm // d, ATT_OUT), lambda i: (i // per_b, 0, i % per_b, 0))
        in_specs += [spec, spec]
        args += [o, l]
    return pl.pallas_call(
        functools.partial(_merge_kernel, tm=tm),
        grid=(t // tm,),
        in_specs=in_specs,
        out_specs=pl.BlockSpec((tm, ATT_OUT), lambda i: (i, 0)),
        out_shape=jax.ShapeDtypeStruct((t, ATT_OUT), BF16),
        scratch_shapes=[pltpu.VMEM((ATT_OUT // LANES, tm, LANES), F32)] * 2,
        compiler_params=_cparams(("arbitrary",)),
        name="merge",
    )(*args)


HALO = 8
SSD_SUB = 2
SSD_ROWS = SSD_SUB * SSD_CHUNK


def _expand_rows(fac, col0, is_a):
    parts = []
    for j in range(SSD_HPG // 2):
        a = fac[:, col0 + 2 * j:col0 + 2 * j + 1]
        b = fac[:, col0 + 2 * j + 1:col0 + 2 * j + 2]
        parts.append(jnp.where(is_a, a, b))
    return jnp.concatenate(parts, axis=1)


def _split2(v):
    hi = v.astype(BF16)
    lo = (v - hi.astype(F32)).astype(BF16)
    return jnp.concatenate([hi, lo], axis=1)


def _split3(v):
    hi = v.astype(BF16)
    r1 = v - hi.astype(F32)
    mid = r1.astype(BF16)
    lo = (r1 - mid.astype(F32)).astype(BF16)
    return jnp.concatenate([hi, mid, lo], axis=1)


def _ssd_kernel(z_ref, xbc_ref, prev_ref, next_ref, dt_ref, cw_ref, cb_ref, dtb_ref, alog_ref,
                dskip_ref, nw_ref, out_ref, ext_ref, xc_ref, yf_ref, st_ref, sel_ref, *, n_blocks):
    q = SSD_CHUNK
    p = pl.program_id(1)
    c = pl.program_id(2)
    fwd = p == 0
    cc = jnp.where(fwd, c, n_blocks - 1 - c)
    row0 = pl.multiple_of(cc * SSD_ROWS, SSD_ROWS)

    @pl.when(jnp.logical_and(fwd, c == 0))
    def _():
        yf_ref[...] = jnp.zeros_like(yf_ref)

    @pl.when(c == 0)
    def _():
        st_ref[...] = jnp.zeros_like(st_ref)
        src = lax.broadcasted_iota(jnp.int32, (2 * LANES, SSD_INNER), 0) & (LANES - 1)
        dst = lax.shift_right_logical(lax.broadcasted_iota(jnp.int32, (2 * LANES, SSD_INNER), 1),
                                      SSD_HEAD_DIM.bit_length() - 1)
        sel_ref[...] = jnp.where(src == dst, 1.0, 0.0).astype(BF16)

    @pl.when(fwd)
    def _():
        for cb in range(CONV_CH // LANES):
            cols = slice(cb * LANES, (cb + 1) * LANES)
            ext_ref[cb, 0:HALO, :] = jnp.where(cc == 0, 0.0, prev_ref[:, cols])
            ext_ref[cb, HALO:HALO + SSD_ROWS, :] = xbc_ref[:, cols]
            ext_ref[cb, HALO + SSD_ROWS:, :] = jnp.where(cc == n_blocks - 1, 0.0, next_ref[:, cols])
            acc = jnp.broadcast_to(cb_ref[:, cols], (SSD_ROWS, LANES))
            for k in range(SSD_CONV):
                acc = acc + cw_ref[k:k + 1, cols] * ext_ref[cb, pl.ds(HALO - SSD_CONV // 2 + k, SSD_ROWS), :]
            xc_ref[pl.ds(row0, SSD_ROWS), cols] = _silu(acc)

    gn = SSD_GROUPS * SSD_STATE
    half = SSD_INNER // SSD_GROUPS
    ri = lax.broadcasted_iota(jnp.int32, (q, q), 0)
    ci = lax.broadcasted_iota(jnp.int32, (q, q), 1)
    before = (ci - ri) * jnp.where(fwd, 1, -1) <= 0
    cum_lhs = jnp.concatenate([before.astype(BF16), jnp.ones((q, q), BF16)], axis=0)
    is_a = lax.broadcasted_iota(jnp.int32, (q, LANES), 1) < SSD_HEAD_DIM
    neg_a = -jnp.exp(alog_ref[...])

    for k in range(SSD_SUB):
        off = pl.multiple_of(jnp.where(fwd, k, SSD_SUB - 1 - k) * q, q)
        rows = pl.ds(row0 + off, q)
        xs = xc_ref[rows, 0:SSD_INNER]

        dt_all = _softplus(dt_ref[pl.ds(off, q), :] + dtb_ref[...])
        dta_all = dt_all * neg_a
        dt_d = jnp.where(fwd, dt_all, pltpu.roll(dt_all, LANES - SSD_HEADS, 1))
        dta_d = jnp.where(fwd, dta_all, pltpu.roll(dta_all, LANES - SSD_HEADS, 1))
        sums = jnp.dot(cum_lhs, _split3(dta_d), preferred_element_type=F32)
        sums = sums[:, 0:LANES] + sums[:, LANES:2 * LANES] + sums[:, 2 * LANES:]
        cum = sums[:q]
        total = sums[q:]
        wide = jnp.dot(jnp.concatenate([_split2(jnp.exp(cum)), _split2(jnp.exp(total - cum) * dt_d)], axis=0),
                       sel_ref[...], preferred_element_type=F32)
        e_cum_w = wide[:q]
        d_state_w = wide[q:]
        d_chunk = jnp.exp(total[0:8])
        log_dt_t = jnp.log(dt_d).T
        shifted_t = cum.T - log_dt_t

        ys = []
        for g in range(SSD_GROUPS):
            col0 = g * SSD_HPG
            gcols = slice(g * half, (g + 1) * half)
            xg = xs[:, gcols]
            bg = xc_ref[rows, SSD_INNER + g * SSD_STATE:SSD_INNER + (g + 1) * SSD_STATE]
            cg = xc_ref[rows, SSD_INNER + gn + g * SSD_STATE:SSD_INNER + gn + (g + 1) * SSD_STATE]
            xg16 = xg.astype(BF16)
            bg16 = bg.astype(BF16)
            cg16 = cg.astype(BF16)
            cbm = lax.dot_general(cg16, bg16, (((1,), (1,)), ((), ())), preferred_element_type=F32)
            cbm = jnp.where(before, cbm, 0.0)
            y_parts = []
            for j in range(SSD_HPG // 2):
                ms = []
                for h in (2 * j, 2 * j + 1):
                    col = col0 + h
                    diff = cum[:, col:col + 1] - shifted_t[col:col + 1, :]
                    ms.append((cbm * jnp.exp(jnp.minimum(diff, log_dt_t[col:col + 1, :]))).astype(BF16))
                xp = xg16[:, j * LANES:(j + 1) * LANES]
                zero = jnp.zeros_like(xp)
                xdiag = jnp.concatenate([jnp.where(is_a, xp, zero), jnp.where(is_a, zero, xp)], axis=0)
                y_parts.append(jnp.dot(jnp.concatenate(ms, axis=1), xdiag, preferred_element_type=F32))
            y_diag = jnp.concatenate(y_parts, axis=1)
            st = st_ref[g]
            y_off = jnp.dot(cg16, st.astype(BF16), preferred_element_type=F32) * e_cum_w[:, gcols]
            xd = (xg * d_state_w[:, gcols]).astype(BF16)
            new = jnp.dot(bg.T.astype(BF16), xd, preferred_element_type=F32)
            keep = _expand_rows(d_chunk, col0, is_a[0:8])
            st_ref[g] = (st.reshape(SSD_STATE // 8, 8, -1) * keep[None]).reshape(st.shape) + new
            ys.append(y_diag + y_off)
        y = jnp.concatenate(ys, axis=1)
        yf_ref[rows, :] += y

    @pl.when(jnp.logical_not(fwd))
    def _():
        rows = pl.ds(row0, SSD_ROWS)
        yt = (yf_ref[rows, :] + xc_ref[rows, 0:SSD_INNER] * dskip_ref[...]) * _silu(z_ref[...])
        for g in range(SSD_GROUPS):
            v = yt[:, g * half:(g + 1) * half]
            ms = jnp.mean(v * v, axis=-1, keepdims=True)
            out_ref[:, g * half:(g + 1) * half] = (
                v * lax.rsqrt(ms + RMS_EPS) * nw_ref[:, g * half:(g + 1) * half]).astype(BF16)


def _ssd(zux, dt, cw, cb, dtb, alog, dskip, nw, *, batch, seq):
    t = batch * seq
    rows = SSD_ROWS
    n_blocks = seq // rows
    hb = rows // HALO

    def row_blk(b, p, c):
        return b * n_blocks + jnp.where(p == 0, c, n_blocks - 1 - c)

    def conv_blk(b, p, c):
        return b * n_blocks + jnp.where(p == 0, c, n_blocks - 1)

    def prev_blk(b, p, c):
        return jnp.maximum(conv_blk(b, p, c) * hb - 1, b * n_blocks * hb)

    def next_blk(b, p, c):
        return jnp.minimum((conv_blk(b, p, c) + 1) * hb, (b + 1) * n_blocks * hb - 1)

    def z_blk(b, p, c):
        return jnp.where(p == 0, b * n_blocks + n_blocks - 1, row_blk(b, p, c))

    def const(shape):
        return pl.BlockSpec(shape, lambda b, p, c: (0, 0))

    return pl.pallas_call(
        functools.partial(_ssd_kernel, n_blocks=n_blocks),
        grid=(batch, 2, n_blocks),
        in_specs=[
            pl.BlockSpec((rows, SSD_INNER), lambda b, p, c: (z_blk(b, p, c), 0)),
            pl.BlockSpec((rows, CONV_CH), lambda b, p, c: (conv_blk(b, p, c), 1)),
            pl.BlockSpec((HALO, CONV_CH), lambda b, p, c: (prev_blk(b, p, c), 1)),
            pl.BlockSpec((HALO, CONV_CH), lambda b, p, c: (next_blk(b, p, c), 1)),
            pl.BlockSpec((rows, LANES), lambda b, p, c: (row_blk(b, p, c), 0)),
            const((8, CONV_CH)), const((1, CONV_CH)), const((1, LANES)), const((1, LANES)),
            const((1, SSD_INNER)), const((1, SSD_INNER)),
        ],
        out_specs=pl.BlockSpec((rows, SSD_INNER), lambda b, p, c: (z_blk(b, p, c), 0)),
        out_shape=jax.ShapeDtypeStruct((t, SSD_INNER), BF16),
        scratch_shapes=[
            pltpu.VMEM((CONV_CH // LANES, rows + 2 * HALO, LANES), F32),
            pltpu.VMEM((seq, CONV_CH), F32),
            pltpu.VMEM((seq, SSD_INNER), F32),
            pltpu.VMEM((SSD_GROUPS, SSD_STATE, SSD_INNER // SSD_GROUPS), F32),
            pltpu.VMEM((2 * LANES, SSD_INNER), BF16),
        ],
        compiler_params=_cparams(("arbitrary", "arbitrary", "arbitrary")),
        name="ssd",
    )(zux, zux, zux, zux, dt, cw, cb, dtb, alog, dskip, nw)


POOL_PAD = 16
POOL_ROWS = 512


def _pool_kernel(u_ref, w_ref, sc_ref, out_ref, pad_ref, *, seq):
    zeros = jnp.zeros((POOL_PAD, POOL_GROUP), F32)
    for g in range(len(POOL_WINDOWS)):
        pad_ref[g, 0:POOL_PAD, :] = zeros
        pad_ref[g, POOL_PAD + seq:, :] = zeros
        pad_ref[g, POOL_PAD:POOL_PAD + seq, :] = u_ref[:, g * POOL_GROUP:(g + 1) * POOL_GROUP]
    for r0 in range(0, seq, POOL_ROWS):
        pos = r0 + lax.broadcasted_iota(jnp.int32, (POOL_ROWS, POOL_GROUP), 0)
        for g, w in enumerate(POOL_WINDOWS):
            cols = slice(g * POOL_GROUP, (g + 1) * POOL_GROUP)
            u = pad_ref[g, POOL_PAD + r0:POOL_PAD + r0 + POOL_ROWS, :]
            s = None
            for j in range(-(w // 2), w // 2):
                term = pad_ref[g, POOL_PAD + r0 + j:POOL_PAD + r0 + j + POOL_ROWS, :]
                s = term if s is None else s + term
            cnt = jnp.minimum(pos + w // 2, seq) - jnp.maximum(pos - w // 2, 0)
            mean = s / cnt.astype(F32)
            y = jnp.dot((mean - u).astype(BF16), w_ref[g], preferred_element_type=F32)
            out_ref[r0:r0 + POOL_ROWS, cols] = (y * sc_ref[:, cols]).astype(BF16)


def _pool(zux, pw, psc, *, batch, seq):
    t = batch * seq
    return pl.pallas_call(
        functools.partial(_pool_kernel, seq=seq),
        scratch_shapes=[pltpu.VMEM((len(POOL_WINDOWS), seq + 2 * POOL_PAD, POOL_GROUP), F32)],
        grid=(batch,),
        in_specs=[
            pl.BlockSpec((seq, POOL_WIDTH), lambda b: (b, SSD_INNER // POOL_WIDTH)),
            pl.BlockSpec((len(POOL_WINDOWS), POOL_GROUP, POOL_GROUP), lambda b: (0, 0, 0)),
            pl.BlockSpec((1, POOL_WIDTH), lambda b: (0, 0)),
        ],
        out_specs=pl.BlockSpec((seq, POOL_WIDTH), lambda b: (b, 0)),
        out_shape=jax.ShapeDtypeStruct((t, POOL_WIDTH), BF16),
        compiler_params=_cparams(("arbitrary",)),
        name="pool",
    )(zux, pw, psc)


def _outproj_kernel(x_ref, a_ref, s_ref, p_ref, w_ref, o_ref):
    s0, p0 = ATT_OUT, ATT_OUT + SSD_INNER
    acc = jnp.dot(a_ref[...], w_ref[0, 0:s0, :], preferred_element_type=F32)
    acc = acc + jnp.dot(s_ref[...], w_ref[0, s0:p0, :], preferred_element_type=F32)
    acc = acc + jnp.dot(p_ref[...], w_ref[0, p0:, :], preferred_element_type=F32)
    o_ref[...] = x_ref[...] + acc


def _outproj(x2, att, ssd, pool, wo, layer, *, tm, tn):
    t = x2.shape[0]
    return pl.pallas_call(
        _outproj_kernel,
        grid=(D_MODEL // tn, t // tm),
        in_specs=[
            pl.BlockSpec((tm, tn), lambda j, i: (i, j)),
            pl.BlockSpec((tm, ATT_OUT), lambda j, i: (i, 0)),
            pl.BlockSpec((tm, SSD_INNER), lambda j, i: (i, 0)),
            pl.BlockSpec((tm, POOL_WIDTH), lambda j, i: (i, 0)),
            pl.BlockSpec((1, MIX_WIDTH, tn), lambda j, i: (layer, 0, j)),
        ],
        out_specs=pl.BlockSpec((tm, tn), lambda j, i: (i, j)),
        out_shape=jax.ShapeDtypeStruct((t, D_MODEL), F32),
        compiler_params=_cparams(("arbitrary", "arbitrary")),
        name="outproj",
    )(x2, att, ssd, pool, wo)


def _ffn_kernel(x_ref, nw_ref, wg_ref, wu_ref, wd_ref, fw_ref, o_ref, h_ref, *, n_ff, final):
    j = pl.program_id(1)

    @pl.when(j == 0)
    def _():
        x = x_ref[...]
        ms = jnp.mean(x * x, axis=-1, keepdims=True)
        h_ref[...] = (x * lax.rsqrt(ms + RMS_EPS) * nw_ref[...]).astype(BF16)
        o_ref[...] = x

    h = h_ref[...]
    gate = jnp.dot(h, wg_ref[0], preferred_element_type=F32)
    up = jnp.dot(h, wu_ref[0], preferred_element_type=F32)
    act = (_silu(gate) * up).astype(BF16)
    o_ref[...] += jnp.dot(act, wd_ref[0], preferred_element_type=F32)

    if final:
        @pl.when(j == n_ff - 1)
        def _():
            y = o_ref[...]
            ms = jnp.mean(y * y, axis=-1, keepdims=True)
            o_ref[...] = y * lax.rsqrt(ms + RMS_EPS) * fw_ref[...]


def _ffn(x2, nw, wg, wu, wd, fw, layer, *, tm, tf, final):
    t = x2.shape[0]
    n_ff = D_FF // tf
    return pl.pallas_call(
        functools.partial(_ffn_kernel, n_ff=n_ff, final=final),
        grid=(t // tm, n_ff),
        in_specs=[
            pl.BlockSpec((tm, D_MODEL), lambda i, j: (i, 0)),
            pl.BlockSpec((1, D_MODEL), lambda i, j: (0, 0)),
            pl.BlockSpec((1, D_MODEL, tf), lambda i, j: (layer, 0, j)),
            pl.BlockSpec((1, D_MODEL, tf), lambda i, j: (layer, 0, j)),
            pl.BlockSpec((1, tf, D_MODEL), lambda i, j: (layer, j, 0)),
            pl.BlockSpec((1, D_MODEL), lambda i, j: (0, 0)),
        ],
        out_specs=pl.BlockSpec((tm, D_MODEL), lambda i, j: (i, 0)),
        out_shape=jax.ShapeDtypeStruct((t, D_MODEL), F32),
        scratch_shapes=[pltpu.VMEM((tm, D_MODEL), BF16)],
        compiler_params=_cparams(("arbitrary", "arbitrary")),
        name="ffn_final" if final else "ffn",
    )(x2, nw, wg, wu, wd, fw)


def _pack_in_weights(w_in):
    q0, k0, v0 = 0, ATT_WIDTH, 2 * ATT_WIDTH
    z0 = 3 * ATT_WIDTH
    xbc0 = z0 + SSD_INNER
    dt0 = xbc0 + CONV_CH
    u0 = dt0 + 2 * SSD_HEADS
    tiles = []
    for g in range(N_DIL):
        sl = slice(g * ATT_OUT, (g + 1) * ATT_OUT)
        tiles += [w_in[:, q0:k0][:, sl], w_in[:, k0:v0][:, sl], w_in[:, v0:z0][:, sl]]
    tiles += [w_in[:, z0:xbc0], w_in[:, u0:u0 + POOL_WIDTH], w_in[:, xbc0:dt0]]
    wcat = jnp.concatenate(tiles, axis=1).astype(BF16)
    wdt = jnp.pad(w_in[:, dt0:u0], ((0, 0), (0, LANES - 2 * SSD_HEADS))).astype(BF16)
    return wcat, wdt


def _pad_lanes(v):
    return jnp.pad(v.reshape(1, -1).astype(F32), ((0, 0), (0, LANES - v.size)))


def _layer(x2, slopes, p, fw, *, batch, seq, final):
    wcat, wdt = _pack_in_weights(p["w_in"])
    q0, q1, q2, zux, dt = _inproj(x2, p["norm1_w"].reshape(1, -1), wcat, wdt,
                                  batch=batch, seq=seq, tm=IN_TM)
    ol = [_attention_group(qkv, slopes[g], dil=d) for g, (d, qkv) in enumerate(zip(DILATIONS, (q0, q1, q2)))]
    att = _merge(ol, batch=batch, seq=seq, tm=512)
    cw = jnp.pad(p["conv_w"].astype(F32), ((0, 8 - SSD_CONV), (0, 0)))
    ssd = _ssd(zux, dt, cw, p["conv_b"].reshape(1, -1).astype(F32), _pad_lanes(p["dt_bias"]),
               _pad_lanes(p["a_log"]), jnp.repeat(p["d_skip"].astype(F32), SSD_HEAD_DIM).reshape(1, -1),
               p["ssd_norm_w"].reshape(1, -1).astype(F32), batch=batch, seq=seq)
    pool = _pool(zux, p["pool_w"].astype(BF16), p["pool_scale"].reshape(1, -1).astype(F32),
                 batch=batch, seq=seq)
    x2 = _outproj(x2, att, ssd, pool, p["w_out"], p["layer"], tm=OUT_TM, tn=OUT_TN)
    return _ffn(x2, p["norm2_w"].reshape(1, -1), p["w_gate"], p["w_up"], p["w_down"], fw.reshape(1, -1),
                p["layer"], tm=FFN_TM, tf=FFN_TF, final=final)


def kernel(x, norm1_w, w_in, conv_w, conv_b, dt_bias, a_log, d_skip, ssd_norm_w, pool_w, pool_scale,
           w_out, norm2_w, w_gate, w_up, w_down, final_norm_w):
    batch, seq, _ = x.shape
    depth = w_in.shape[0]
    k = jnp.arange(1, ATT_HEADS + 1, dtype=F32)
    slopes = (2.0 ** (-8.0 * k / ATT_HEADS)).reshape(N_DIL, ATT_SLOTS)
    x2 = x.reshape(batch * seq, D_MODEL)
    wo, wg, wu, wd = (w.astype(BF16) for w in (w_out, w_gate, w_up, w_down))
    for l in range(depth):
        p = dict(norm1_w=norm1_w[l], w_in=w_in[l], layer=l, conv_w=conv_w[l], conv_b=conv_b[l], dt_bias=dt_bias[l],
                 a_log=a_log[l], d_skip=d_skip[l], ssd_norm_w=ssd_norm_w[l], pool_w=pool_w[l],
                 pool_scale=pool_scale[l], w_out=wo, norm2_w=norm2_w[l], w_gate=wg, w_up=wu, w_down=wd)
        x2 = _layer(x2, slopes, p, final_norm_w, batch=batch, seq=seq, final=(l == depth - 1))
    return x2.reshape(batch, seq, D_MODEL)
```

```python
import functools
import math

import jax
import jax.numpy as jnp
from jax import lax
from jax.experimental import pallas as pl
from jax.experimental.pallas import tpu as pltpu

F32 = jnp.float32
BF16 = jnp.bfloat16

D_MODEL = 2048
ATT_HEAD_DIM = 64
ATT_SLOTS = 8
DILATIONS = (1, 4, 16)
ATT_HALF = 64
N_DIL = len(DILATIONS)
ATT_HEADS = ATT_SLOTS * N_DIL
ATT_WIDTH = ATT_HEADS * ATT_HEAD_DIM
ATT_OUT = ATT_SLOTS * ATT_HEAD_DIM
SSD_HEAD_DIM = 64
SSD_INNER = D_MODEL // 2
SSD_HEADS = SSD_INNER // SSD_HEAD_DIM
SSD_GROUPS = 2
SSD_HPG = SSD_HEADS // SSD_GROUPS
SSD_STATE = 128
SSD_CONV = 5
SSD_CHUNK = 128
CONV_CH = SSD_INNER + 2 * SSD_GROUPS * SSD_STATE
POOL_WINDOWS = (2, 4, 8, 16)
POOL_GROUP = 128
POOL_WIDTH = POOL_GROUP * len(POOL_WINDOWS)
MIX_WIDTH = ATT_OUT + SSD_INNER + POOL_WIDTH
D_FF = 5632
RMS_EPS = 1e-6
NEG_INF = -1e30

LANES = 128
IN_TILE = 768
IN_TILES_PER_GROUP = 2
N_IN_TILES = 5 * IN_TILES_PER_GROUP
IN_TM = 1024
VMEM_LIMIT = 56 * 1024 * 1024
OUT_TM = 512
OUT_TN = 2048
FFN_TM = 1024
FFN_TF = 512


def _cparams(sem):
    return pltpu.CompilerParams(dimension_semantics=sem, vmem_limit_bytes=VMEM_LIMIT)


def _silu(x):
    return x * (1.0 / (1.0 + jnp.exp(-x)))


def _softplus(x):
    return jnp.maximum(x, 0.0) + jnp.log(1.0 + jnp.exp(-jnp.abs(x)))


def _inproj_kernel(x_ref, nw_ref, w_ref, wdt_ref, q0_ref, q1_ref, q2_ref, zux_ref, dt_ref,
                   h_ref, stage_ref, *, tm):
    j = pl.program_id(1)
    jg = j // IN_TILES_PER_GROUP

    @pl.when(j == 0)
    def _():
        x = x_ref[...]
        ms = jnp.mean(x * x, axis=-1, keepdims=True)
        h = x * lax.rsqrt(ms + RMS_EPS) * nw_ref[...]
        h_ref[...] = h.astype(BF16)
        dt_ref[...] = jnp.dot(h_ref[...], wdt_ref[...], preferred_element_type=F32)

    def proj():
        return jnp.dot(h_ref[...], w_ref[...], preferred_element_type=F32)

    @pl.when(jg == 0)
    def _():
        q0_ref[0, 0] = proj().astype(BF16)

    for g, (d, out_ref) in enumerate(zip(DILATIONS, (q0_ref, q1_ref, q2_ref))):
        if g == 0:
            continue

        @pl.when(jg == g)
        def _(d=d, out_ref=out_ref):
            res = proj()
            for cb in range(IN_TILE // LANES):
                stage_ref[cb] = res[:, cb * LANES:(cb + 1) * LANES]
            for r in range(d):
                for cb in range(IN_TILE // LANES):
                    out_ref[0, r, :, cb * LANES:(cb + 1) * LANES] = (
                        stage_ref[cb, pl.ds(r, tm // d, stride=d), :].astype(BF16))

    @pl.when(jg >= N_DIL)
    def _():
        zux_ref[...] = proj()


def _inproj(x2, nw, wcat, wdt, *, batch, seq, tm):
    t = x2.shape[0]
    per_b = seq // tm
    grid = (t // tm, N_IN_TILES)
    tpg = IN_TILES_PER_GROUP

    def qspec(g, d):
        return pl.BlockSpec((1, d, tm // d, IN_TILE),
                            lambda i, j: (i // per_b, 0, i % per_b, jnp.clip(j - g * tpg, 0, tpg - 1)))

    out_shape = [jax.ShapeDtypeStruct((batch, d, seq // d, tpg * IN_TILE), BF16) for d in DILATIONS]
    out_shape += [jax.ShapeDtypeStruct((t, 2 * tpg * IN_TILE), F32), jax.ShapeDtypeStruct((t, LANES), F32)]
    return pl.pallas_call(
        functools.partial(_inproj_kernel, tm=tm),
        grid=grid,
        in_specs=[
            pl.BlockSpec((tm, D_MODEL), lambda i, j: (i, 0)),
            pl.BlockSpec((1, D_MODEL), lambda i, j: (0, 0)),
            pl.BlockSpec((D_MODEL, IN_TILE), lambda i, j: (0, j)),
            pl.BlockSpec((D_MODEL, LANES), lambda i, j: (0, 0)),
        ],
        out_specs=[qspec(g, d) for g, d in enumerate(DILATIONS)] + [
            pl.BlockSpec((tm, IN_TILE), lambda i, j: (i, jnp.maximum(j - N_DIL * tpg, 0))),
            pl.BlockSpec((tm, LANES), lambda i, j: (i, 0)),
        ],
        out_shape=out_shape,
        scratch_shapes=[pltpu.VMEM((tm, D_MODEL), BF16), pltpu.VMEM((IN_TILE // LANES, tm, LANES), F32)],
        compiler_params=_cparams(("arbitrary", "arbitrary")),
        name="inproj",
    )(x2, nw, wcat, wdt)


ATT_QB = 128
ATT_KW = 256
ATT_UNROLL = 16


def _attn_kernel(slopes_ref, q_ref, k_ref, v_ref, o_ref, lse_ref, bias_ref, *, seq_len, dil):
    sp = pl.program_id(0)
    is_a = lax.broadcasted_iota(jnp.int32, (ATT_QB, LANES), 1) < ATT_HEAD_DIM
    n_blocks = seq_len // ATT_QB

    @pl.when(pl.program_id(1) == 0)
    def _():
        slope_a = slopes_ref[2 * sp] * float(dil)
        slope_b = slopes_ref[2 * sp + 1] * float(dil)
        rows = lax.broadcasted_iota(jnp.int32, (2 * ATT_QB, ATT_KW), 0)
        cols = lax.broadcasted_iota(jnp.int32, (2 * ATT_QB, ATT_KW), 1)
        rel_base = cols - (rows & (ATT_QB - 1))
        slope = jnp.where(rows < ATT_QB, slope_a, slope_b)
        for t in range(3):
            dist = jnp.abs(rel_base - t * ATT_HALF)
            bias_ref[t] = jnp.where(dist <= ATT_HALF, -slope * dist.astype(F32), NEG_INF)

    def body(it, carry):
        r = lax.shift_right_logical(it, n_blocks.bit_length() - 1)
        i = it & (n_blocks - 1)
        q0 = pl.multiple_of(i * ATT_QB, ATT_QB)
        k0 = pl.multiple_of(jnp.clip(i * ATT_QB - ATT_HALF, 0, seq_len - ATT_KW), ATT_HALF)
        q2 = q_ref[0, r, pl.ds(q0, ATT_QB), :] * jnp.asarray(ATT_HEAD_DIM ** -0.5, BF16)
        zero = jnp.zeros_like(q2)
        qs = jnp.concatenate([jnp.where(is_a, q2, zero), jnp.where(is_a, zero, q2)], axis=0)
        kk = k_ref[0, r, pl.ds(k0, ATT_KW), :]
        vv = v_ref[0, r, pl.ds(k0, ATT_KW), :]
        s = lax.dot_general(qs, kk, (((1,), (1,)), ((), ())), preferred_element_type=F32)
        s = s + bias_ref[lax.shift_right_logical(q0 - k0, ATT_HALF.bit_length() - 1)]
        m = jnp.max(s, axis=-1, keepdims=True)
        p = jnp.exp(s - m)
        l = jnp.sum(p, axis=-1, keepdims=True)
        pv = jnp.dot(p.astype(BF16), vv, preferred_element_type=F32)
        o = pv * (1.0 / l)
        lse = jnp.broadcast_to(m + jnp.log(l), (2 * ATT_QB, LANES))
        o_ref[0, r, pl.ds(q0, ATT_QB), :] = jnp.where(is_a, o[:ATT_QB], o[ATT_QB:]).astype(BF16)
        lse_ref[0, r, pl.ds(q0, ATT_QB), :] = jnp.where(is_a, lse[:ATT_QB], lse[ATT_QB:])
        return carry

    lax.fori_loop(0, dil * n_blocks, body, 0, unroll=ATT_UNROLL)


def _attention_group(qkv, slopes, *, dil):
    batch, d, seq_len, _ = qkv.shape
    n_pairs = ATT_OUT // LANES

    def spec(col0):
        return pl.BlockSpec((1, d, seq_len, LANES), lambda sp, b: (b, 0, 0, col0 + sp))

    shape = (batch, d, seq_len, ATT_OUT)
    return pl.pallas_call(
        functools.partial(_attn_kernel, seq_len=seq_len, dil=dil),
        grid=(n_pairs, batch),
        in_specs=[pl.BlockSpec(memory_space=pltpu.SMEM), spec(0), spec(n_pairs), spec(2 * n_pairs)],
        out_specs=[spec(0), spec(0)],
        out_shape=[jax.ShapeDtypeStruct(shape, BF16), jax.ShapeDtypeStruct(shape, F32)],
        scratch_shapes=[pltpu.VMEM((3, 2 * ATT_QB, ATT_KW), F32)],
        compiler_params=_cparams(("arbitrary", "arbitrary")),
        name=f"attn_d{dil}",
    )(slopes, qkv, qkv, qkv)


def _merge_kernel(o0_ref, l0_ref, o1_ref, l1_ref, o2_ref, l2_ref, att_ref, so_ref, sl_ref, *, tm):
    outs = [o0_ref[0, 0].astype(F32)]
    lses = [l0_ref[0, 0]]
    for d, o_ref, l_ref in ((DILATIONS[1], o1_ref, l1_ref), (DILATIONS[2], o2_ref, l2_ref)):
        for r in range(d):
            for cb in range(ATT_OUT // LANES):
                cols = slice(cb * LANES, (cb + 1) * LANES)
                so_ref[cb, pl.ds(r, tm // d, stride=d), :] = o_ref[0, r, :, cols].astype(F32)
                sl_ref[cb, pl.ds(r, tm // d, stride=d), :] = l_ref[0, r, :, cols]
        outs.append(jnp.concatenate([so_ref[cb] for cb in range(ATT_OUT // LANES)], axis=1))
        lses.append(jnp.concatenate([sl_ref[cb] for cb in range(ATT_OUT // LANES)], axis=1))
    mx = jnp.maximum(jnp.maximum(lses[0], lses[1]), lses[2])
    ws = [jnp.exp(l - mx) for l in lses]
    num = ws[0] * outs[0] + ws[1] * outs[1] + ws[2] * outs[2]
    att_ref[...] = (num / (ws[0] + ws[1] + ws[2])).astype(BF16)


def _merge(ol, *, batch, seq, tm):
    t = batch * seq
    per_b = seq // tm
    in_specs = []
    args = []
    for d, (o, l) in zip(DILATIONS, ol):
        spec = pl.BlockSpec((1, d, tm // d, ATT_OUT), lambda i: (i // per_b, 0, i % per_b, 0))
        in_specs += [spec, spec]
        args += [o, l]
    return pl.pallas_call(
        functools.partial(_merge_kernel, tm=tm),
        grid=(t // tm,),
        in_specs=in_specs,
        out_specs=pl.BlockSpec((tm, ATT_OUT), lambda i: (i, 0)),
        out_shape=jax.ShapeDtypeStruct((t, ATT_OUT), BF16),
        scratch_shapes=[pltpu.VMEM((ATT_OUT // LANES, tm, LANES), F32)] * 2,
        compiler_params=_cparams(("arbitrary",)),
        name="merge",
    )(*args)


HALO = 8
SSD_SUB = 2
SSD_ROWS = SSD_SUB * SSD_CHUNK


def _expand_rows(fac, col0, is_a):
    parts = []
    for j in range(SSD_HPG // 2):
        a = fac[:, col0 + 2 * j:col0 + 2 * j + 1]
        b = fac[:, col0 + 2 * j + 1:col0 + 2 * j + 2]
        parts.append(jnp.where(is_a, a, b))
    return jnp.concatenate(parts, axis=1)


def _split2(v):
    hi = v.astype(BF16)
    lo = (v - hi.astype(F32)).astype(BF16)
    return jnp.concatenate([hi, lo], axis=1)


def _split3(v):
    hi = v.astype(BF16)
    r1 = v - hi.astype(F32)
    mid = r1.astype(BF16)
    lo = (r1 - mid.astype(F32)).astype(BF16)
    return jnp.concatenate([hi, mid, lo], axis=1)


def _ssd_kernel(z_ref, xbc_ref, prev_ref, next_ref, dt_ref, cw_ref, cb_ref, dtb_ref, alog_ref,
                dskip_ref, nw_ref, out_ref, ext_ref, xc_ref, yf_ref, st_ref, sel_ref, *, n_blocks):
    q = SSD_CHUNK
    p = pl.program_id(1)
    c = pl.program_id(2)
    fwd = p == 0
    cc = jnp.where(fwd, c, n_blocks - 1 - c)
    row0 = pl.multiple_of(cc * SSD_ROWS, SSD_ROWS)

    @pl.when(jnp.logical_and(fwd, c == 0))
    def _():
        yf_ref[...] = jnp.zeros_like(yf_ref)

    @pl.when(c == 0)
    def _():
        st_ref[...] = jnp.zeros_like(st_ref)
        src = lax.broadcasted_iota(jnp.int32, (2 * LANES, SSD_INNER), 0) & (LANES - 1)
        dst = lax.shift_right_logical(lax.broadcasted_iota(jnp.int32, (2 * LANES, SSD_INNER), 1),
                                      SSD_HEAD_DIM.bit_length() - 1)
        sel_ref[...] = jnp.where(src == dst, 1.0, 0.0).astype(BF16)

    @pl.when(fwd)
    def _():
        for cb in range(CONV_CH // LANES):
            cols = slice(cb * LANES, (cb + 1) * LANES)
            ext_ref[cb, 0:HALO, :] = jnp.where(cc == 0, 0.0, prev_ref[:, cols])
            ext_ref[cb, HALO:HALO + SSD_ROWS, :] = xbc_ref[:, cols]
            ext_ref[cb, HALO + SSD_ROWS:, :] = jnp.where(cc == n_blocks - 1, 0.0, next_ref[:, cols])
            acc = jnp.broadcast_to(cb_ref[:, cols], (SSD_ROWS, LANES))
            for k in range(SSD_CONV):
                acc = acc + cw_ref[k:k + 1, cols] * ext_ref[cb, pl.ds(HALO - SSD_CONV // 2 + k, SSD_ROWS), :]
            xc_ref[pl.ds(row0, SSD_ROWS), cols] = _silu(acc)

    gn = SSD_GROUPS * SSD_STATE
    half = SSD_INNER // SSD_GROUPS
    ri = lax.broadcasted_iota(jnp.int32, (q, q), 0)
    ci = lax.broadcasted_iota(jnp.int32, (q, q), 1)
    before = (ci - ri) * jnp.where(fwd, 1, -1) <= 0
    cum_lhs = jnp.concatenate([before.astype(BF16), jnp.ones((q, q), BF16)], axis=0)
    is_a = lax.broadcasted_iota(jnp.int32, (q, LANES), 1) < SSD_HEAD_DIM
    neg_a = -jnp.exp(alog_ref[...])

    for k in range(SSD_SUB):
        off = pl.multiple_of(jnp.where(fwd, k, SSD_SUB - 1 - k) * q, q)
        rows = pl.ds(row0 + off, q)
        xs = xc_ref[rows, 0:SSD_INNER]

        dt_all = _softplus(dt_ref[pl.ds(off, q), :] + dtb_ref[...])
        dta_all = dt_all * neg_a
        dt_d = jnp.where(fwd, dt_all, pltpu.roll(dt_all, LANES - SSD_HEADS, 1))
        dta_d = jnp.where(fwd, dta_all, pltpu.roll(dta_all, LANES - SSD_HEADS, 1))
        sums = jnp.dot(cum_lhs, _split3(dta_d), preferred_element_type=F32)
        sums = sums[:, 0:LANES] + sums[:, LANES:2 * LANES] + sums[:, 2 * LANES:]
        cum = sums[:q]
        total = sums[q:]
        wide = jnp.dot(jnp.concatenate([_split2(jnp.exp(cum)), _split2(jnp.exp(total - cum) * dt_d)], axis=0),
                       sel_ref[...], preferred_element_type=F32)
        e_cum_w = wide[:q]
        d_state_w = wide[q:]
        d_chunk = jnp.exp(total[0:8])
        log_dt_t = jnp.log(dt_d).T
        shifted_t = cum.T - log_dt_t

        ys = []
        for g in range(SSD_GROUPS):
            col0 = g * SSD_HPG
            gcols = slice(g * half, (g + 1) * half)
            xg = xs[:, gcols]
            bg = xc_ref[rows, SSD_INNER + g * SSD_STATE:SSD_INNER + (g + 1) * SSD_STATE]
            cg = xc_ref[rows, SSD_INNER + gn + g * SSD_STATE:SSD_INNER + gn + (g + 1) * SSD_STATE]
            xg16 = xg.astype(BF16)
            bg16 = bg.astype(BF16)
            cg16 = cg.astype(BF16)
            cbm = lax.dot_general(cg16, bg16, (((1,), (1,)), ((), ())), preferred_element_type=F32)
            cbm = jnp.where(before, cbm, 0.0)
            y_parts = []
            for j in range(SSD_HPG // 2):
                ms = []
                for h in (2 * j, 2 * j + 1):
                    col = col0 + h
                    diff = cum[:, col:col + 1] - shifted_t[col:col + 1, :]
                    ms.append((cbm * jnp.exp(jnp.minimum(diff, log_dt_t[col:col + 1, :]))).astype(BF16))
                xp = xg16[:, j * LANES:(j + 1) * LANES]
                zero = jnp.zeros_like(xp)
                xdiag = jnp.concatenate([jnp.where(is_a, xp, zero), jnp.where(is_a, zero, xp)], axis=0)
                y_parts.append(jnp.dot(jnp.concatenate(ms, axis=1), xdiag, preferred_element_type=F32))
            y_diag = jnp.concatenate(y_parts, axis=1)
            st = st_ref[g]
            y_off = jnp.dot(cg16, st.astype(BF16), preferred_element_type=F32) * e_cum_w[:, gcols]
            xd = (xg * d_state_w[:, gcols]).astype(BF16)
            new = jnp.dot(bg.T.astype(BF16), xd, preferred_element_type=F32)
            keep = _expand_rows(d_chunk, col0, is_a[0:8])
            st_ref[g] = (st.reshape(SSD_STATE // 8, 8, -1) * keep[None]).reshape(st.shape) + new
            ys.append(y_diag + y_off)
        y = jnp.concatenate(ys, axis=1)
        yf_ref[rows, :] += y

    @pl.when(jnp.logical_not(fwd))
    def _():
        rows = pl.ds(row0, SSD_ROWS)
        yt = (yf_ref[rows, :] + xc_ref[rows, 0:SSD_INNER] * dskip_ref[...]) * _silu(z_ref[...])
        for g in range(SSD_GROUPS):
            v = yt[:, g * half:(g + 1) * half]
            ms = jnp.mean(v * v, axis=-1, keepdims=True)
            out_ref[:, g * half:(g + 1) * half] = (
                v * lax.rsqrt(ms + RMS_EPS) * nw_ref[:, g * half:(g + 1) * half]).astype(BF16)


def _ssd(zux, dt, cw, cb, dtb, alog, dskip, nw, *, batch, seq):
    t = batch * seq
    rows = SSD_ROWS
    n_blocks = seq // rows
    hb = rows // HALO

    def row_blk(b, p, c):
        return b * n_blocks + jnp.where(p == 0, c, n_blocks - 1 - c)

    def conv_blk(b, p, c):
        return b * n_blocks + jnp.where(p == 0, c, n_blocks - 1)

    def prev_blk(b, p, c):
        return jnp.maximum(conv_blk(b, p, c) * hb - 1, b * n_blocks * hb)

    def next_blk(b, p, c):
        return jnp.minimum((conv_blk(b, p, c) + 1) * hb, (b + 1) * n_blocks * hb - 1)

    def z_blk(b, p, c):
        return jnp.where(p == 0, b * n_blocks + n_blocks - 1, row_blk(b, p, c))

    def const(shape):
        return pl.BlockSpec(shape, lambda b, p, c: (0, 0))

    return pl.pallas_call(
        functools.partial(_ssd_kernel, n_blocks=n_blocks),
        grid=(batch, 2, n_blocks),
        in_specs=[
            pl.BlockSpec((rows, SSD_INNER), lambda b, p, c: (z_blk(b, p, c), 0)),
            pl.BlockSpec((rows, CONV_CH), lambda b, p, c: (conv_blk(b, p, c), 1)),
            pl.BlockSpec((HALO, CONV_CH), lambda b, p, c: (prev_blk(b, p, c), 1)),
            pl.BlockSpec((HALO, CONV_CH), lambda b, p, c: (next_blk(b, p, c), 1)),
            pl.BlockSpec((rows, LANES), lambda b, p, c: (row_blk(b, p, c), 0)),
            const((8, CONV_CH)), const((1, CONV_CH)), const((1, LANES)), const((1, LANES)),
            const((1, SSD_INNER)), const((1, SSD_INNER)),
        ],
        out_specs=pl.BlockSpec((rows, SSD_INNER), lambda b, p, c: (z_blk(b, p, c), 0)),
        out_shape=jax.ShapeDtypeStruct((t, SSD_INNER), BF16),
        scratch_shapes=[
            pltpu.VMEM((CONV_CH // LANES, rows + 2 * HALO, LANES), F32),
            pltpu.VMEM((seq, CONV_CH), F32),
            pltpu.VMEM((seq, SSD_INNER), F32),
            pltpu.VMEM((SSD_GROUPS, SSD_STATE, SSD_INNER // SSD_GROUPS), F32),
            pltpu.VMEM((2 * LANES, SSD_INNER), BF16),
        ],
        compiler_params=_cparams(("arbitrary", "arbitrary", "arbitrary")),
        name="ssd",
    )(zux, zux, zux, zux, dt, cw, cb, dtb, alog, dskip, nw)


POOL_PAD = 16
POOL_ROWS = 512


def _pool_kernel(u_ref, w_ref, sc_ref, out_ref, pad_ref, *, seq):
    zeros = jnp.zeros((POOL_PAD, POOL_GROUP), F32)
    for g in range(len(POOL_WINDOWS)):
        pad_ref[g, 0:POOL_PAD, :] = zeros
        pad_ref[g, POOL_PAD + seq:, :] = zeros
        pad_ref[g, POOL_PAD:POOL_PAD + seq, :] = u_ref[:, g * POOL_GROUP:(g + 1) * POOL_GROUP]
    for r0 in range(0, seq, POOL_ROWS):
        pos = r0 + lax.broadcasted_iota(jnp.int32, (POOL_ROWS, POOL_GROUP), 0)
        for g, w in enumerate(POOL_WINDOWS):
            cols = slice(g * POOL_GROUP, (g + 1) * POOL_GROUP)
            u = pad_ref[g, POOL_PAD + r0:POOL_PAD + r0 + POOL_ROWS, :]
            s = None
            for j in range(-(w // 2), w // 2):
                term = pad_ref[g, POOL_PAD + r0 + j:POOL_PAD + r0 + j + POOL_ROWS, :]
                s = term if s is None else s + term
            cnt = jnp.minimum(pos + w // 2, seq) - jnp.maximum(pos - w // 2, 0)
            mean = s / cnt.astype(F32)
            y = jnp.dot((mean - u).astype(BF16), w_ref[g], preferred_element_type=F32)
            out_ref[r0:r0 + POOL_ROWS, cols] = (y * sc_ref[:, cols]).astype(BF16)


def _pool(zux, pw, psc, *, batch, seq):
    t = batch * seq
    return pl.pallas_call(
        functools.partial(_pool_kernel, seq=seq),
        scratch_shapes=[pltpu.VMEM((len(POOL_WINDOWS), seq + 2 * POOL_PAD, POOL_GROUP), F32)],
        grid=(batch,),
        in_specs=[
            pl.BlockSpec((seq, POOL_WIDTH), lambda b: (b, SSD_INNER // POOL_WIDTH)),
            pl.BlockSpec((len(POOL_WINDOWS), POOL_GROUP, POOL_GROUP), lambda b: (0, 0, 0)),
            pl.BlockSpec((1, POOL_WIDTH), lambda b: (0, 0)),
        ],
        out_specs=pl.BlockSpec((seq, POOL_WIDTH), lambda b: (b, 0)),
        out_shape=jax.ShapeDtypeStruct((t, POOL_WIDTH), BF16),
        compiler_params=_cparams(("arbitrary",)),
        name="pool",
    )(zux, pw, psc)


def _outproj_kernel(x_ref, a_ref, s_ref, p_ref, w_ref, o_ref):
    s0, p0 = ATT_OUT, ATT_OUT + SSD_INNER
    acc = jnp.dot(a_ref[...], w_ref[0, 0:s0, :], preferred_element_type=F32)
    acc = acc + jnp.dot(s_ref[...], w_ref[0, s0:p0, :], preferred_element_type=F32)
    acc = acc + jnp.dot(p_ref[...], w_ref[0, p0:, :], preferred_element_type=F32)
    o_ref[...] = x_ref[...] + acc


def _outproj(x2, att, ssd, pool, wo, layer, *, tm, tn):
    t = x2.shape[0]
    return pl.pallas_call(
        _outproj_kernel,
        grid=(D_MODEL // tn, t // tm),
        in_specs=[
            pl.BlockSpec((tm, tn), lambda j, i: (i, j)),
            pl.BlockSpec((tm, ATT_OUT), lambda j, i: (i, 0)),
            pl.BlockSpec((tm, SSD_INNER), lambda j, i: (i, 0)),
            pl.BlockSpec((tm, POOL_WIDTH), lambda j, i: (i, 0)),
            pl.BlockSpec((1, MIX_WIDTH, tn), lambda j, i: (layer, 0, j)),
        ],
        out_specs=pl.BlockSpec((tm, tn), lambda j, i: (i, j)),
        out_shape=jax.ShapeDtypeStruct((t, D_MODEL), F32),
        compiler_params=_cparams(("arbitrary", "arbitrary")),
        name="outproj",
    )(x2, att, ssd, pool, wo)


def _ffn_kernel(x_ref, nw_ref, wg_ref, wu_ref, wd_ref, fw_ref, o_ref, h_ref, *, n_ff, final):
    j = pl.program_id(1)

    @pl.when(j == 0)
    def _():
        x = x_ref[...]
        ms = jnp.mean(x * x, axis=-1, keepdims=True)
        h_ref[...] = (x * lax.rsqrt(ms + RMS_EPS) * nw_ref[...]).astype(BF16)
        o_ref[...] = x

    h = h_ref[...]
    gate = jnp.dot(h, wg_ref[0], preferred_element_type=F32)
    up = jnp.dot(h, wu_ref[0], preferred_element_type=F32)
    act = (_silu(gate) * up).astype(BF16)
    o_ref[...] += jnp.dot(act, wd_ref[0], preferred_element_type=F32)

    if final:
        @pl.when(j == n_ff - 1)
        def _():
            y = o_ref[...]
            ms = jnp.mean(y * y, axis=-1, keepdims=True)
            o_ref[...] = y * lax.rsqrt(ms + RMS_EPS) * fw_ref[...]


def _ffn(x2, nw, wg, wu, wd, fw, layer, *, tm, tf, final):
    t = x2.shape[0]
    n_ff = D_FF // tf
    return pl.pallas_call(
        functools.partial(_ffn_kernel, n_ff=n_ff, final=final),
        grid=(t // tm, n_ff),
        in_specs=[
            pl.BlockSpec((tm, D_MODEL), lambda i, j: (i, 0)),
            pl.BlockSpec((1, D_MODEL), lambda i, j: (0, 0)),
            pl.BlockSpec((1, D_MODEL, tf), lambda i, j: (layer, 0, j)),
            pl.BlockSpec((1, D_MODEL, tf), lambda i, j: (layer, 0, j)),
            pl.BlockSpec((1, tf, D_MODEL), lambda i, j: (layer, j, 0)),
            pl.BlockSpec((1, D_MODEL), lambda i, j: (0, 0)),
        ],
        out_specs=pl.BlockSpec((tm, D_MODEL), lambda i, j: (i, 0)),
        out_shape=jax.ShapeDtypeStruct((t, D_MODEL), F32),
        scratch_shapes=[pltpu.VMEM((tm, D_MODEL), BF16)],
        compiler_params=_cparams(("arbitrary", "arbitrary")),
        name="ffn_final" if final else "ffn",
    )(x2, nw, wg, wu, wd, fw)


PACK_W = 512
PACK_U_TILE = 11


def _pack_kernel(wt_ref, tail_ref, o_ref, wdt_ref):
    j = pl.program_id(0)
    n_dt = 2 * SSD_HEADS

    @pl.when(j == 0)
    def _():
        lane = lax.broadcasted_iota(jnp.int32, (D_MODEL, LANES), 1)
        wdt_ref[...] = jnp.where(lane < n_dt, tail_ref[:, 0:LANES], 0.0).astype(BF16)

    @pl.when(j != PACK_U_TILE)
    def _():
        o_ref[...] = wt_ref[0].T.astype(BF16)

    @pl.when(j == PACK_U_TILE)
    def _():
        o_ref[...] = tail_ref[:, n_dt:n_dt + POOL_WIDTH].astype(BF16)


def _pack_in_weights(w_in, w_in_t, layer):
    z0 = 3 * ATT_WIDTH
    dt0 = z0 + SSD_INNER + CONV_CH
    n_tail = w_in.shape[2] - dt0
    n_qkv = z0 // PACK_W

    def src_tile(j):
        qkv = (j % N_DIL) * 3 + j // N_DIL
        return jnp.where(j < n_qkv, qkv, jnp.where(j < PACK_U_TILE, j, j - 1))

    return pl.pallas_call(
        _pack_kernel,
        grid=(N_IN_TILES * IN_TILE // PACK_W,),
        in_specs=[
            pl.BlockSpec((1, PACK_W, D_MODEL), lambda j: (layer, src_tile(j), 0)),
            pl.BlockSpec((D_MODEL, n_tail), lambda j: (0, 0)),
        ],
        out_specs=[pl.BlockSpec((D_MODEL, PACK_W), lambda j: (0, j)),
                   pl.BlockSpec((D_MODEL, LANES), lambda j: (0, 0))],
        out_shape=[jax.ShapeDtypeStruct((D_MODEL, N_IN_TILES * IN_TILE), BF16),
                   jax.ShapeDtypeStruct((D_MODEL, LANES), BF16)],
        compiler_params=_cparams(("arbitrary",)),
        name="pack_w_in",
    )(w_in_t, w_in[layer, :, dt0:])


def _pad_lanes(v):
    return jnp.pad(v.reshape(1, -1).astype(F32), ((0, 0), (0, LANES - v.size)))


def _layer(x2, slopes, p, fw, *, batch, seq, final):
    wcat, wdt = _pack_in_weights(p["w_in"], p["w_in_t"], p["layer"])
    q0, q1, q2, zux, dt = _inproj(x2, p["norm1_w"].reshape(1, -1), wcat, wdt,
                                  batch=batch, seq=seq, tm=IN_TM)
    ol = [_attention_group(qkv, slopes[g], dil=d) for g, (d, qkv) in enumerate(zip(DILATIONS, (q0, q1, q2)))]
    att = _merge(ol, batch=batch, seq=seq, tm=512)
    cw = jnp.pad(p["conv_w"].astype(F32), ((0, 8 - SSD_CONV), (0, 0)))
    ssd = _ssd(zux, dt, cw, p["conv_b"].reshape(1, -1).astype(F32), _pad_lanes(p["dt_bias"]),
               _pad_lanes(p["a_log"]), jnp.repeat(p["d_skip"].astype(F32), SSD_HEAD_DIM).reshape(1, -1),
               p["ssd_norm_w"].reshape(1, -1).astype(F32), batch=batch, seq=seq)
    pool = _pool(zux, p["pool_w"].astype(BF16), p["pool_scale"].reshape(1, -1).astype(F32),
                 batch=batch, seq=seq)
    x2 = _outproj(x2, att, ssd, pool, p["w_out"], p["layer"], tm=OUT_TM, tn=OUT_TN)
    return _ffn(x2, p["norm2_w"].reshape(1, -1), p["w_gate"], p["w_up"], p["w_down"], fw.reshape(1, -1),
                p["layer"], tm=FFN_TM, tf=FFN_TF, final=final)


def kernel(x, norm1_w, w_in, conv_w, conv_b, dt_bias, a_log, d_skip, ssd_norm_w, pool_w, pool_scale,
           w_out, norm2_w, w_gate, w_up, w_down, final_norm_w):
    batch, seq, _ = x.shape
    depth = w_in.shape[0]
    k = jnp.arange(1, ATT_HEADS + 1, dtype=F32)
    slopes = (2.0 ** (-8.0 * k / ATT_HEADS)).reshape(N_DIL, ATT_SLOTS)
    x2 = x.reshape(batch * seq, D_MODEL)
    wo, wg, wu, wd = (w.astype(BF16) for w in (w_out, w_gate, w_up, w_down))
    w_in_t = jnp.swapaxes(w_in, 1, 2)
    for l in range(depth):
        p = dict(norm1_w=norm1_w[l], w_in=w_in, w_in_t=w_in_t, layer=l, conv_w=conv_w[l], conv_b=conv_b[l], dt_bias=dt_bias[l],
                 a_log=a_log[l], d_skip=d_skip[l], ssd_norm_w=ssd_norm_w[l], pool_w=pool_w[l],
                 pool_scale=pool_scale[l], w_out=wo, norm2_w=norm2_w[l], w_gate=wg, w_up=wu, w_down=wd)
        x2 = _layer(x2, slopes, p, final_norm_w, batch=batch, seq=seq, final=(l == depth - 1))
    return x2.reshape(batch, seq, D_MODEL)
```

```python
import functools
import math

import jax
import jax.numpy as jnp
from jax import lax
from jax.experimental import pallas as pl
from jax.experimental.pallas import tpu as pltpu

F32 = jnp.float32
BF16 = jnp.bfloat16

D_MODEL = 2048
ATT_HEAD_DIM = 64
ATT_SLOTS = 8
DILATIONS = (1, 4, 16)
ATT_HALF = 64
N_DIL = len(DILATIONS)
ATT_HEADS = ATT_SLOTS * N_DIL
ATT_WIDTH = ATT_HEADS * ATT_HEAD_DIM
ATT_OUT = ATT_SLOTS * ATT_HEAD_DIM
SSD_HEAD_DIM = 64
SSD_INNER = D_MODEL // 2
SSD_HEADS = SSD_INNER // SSD_HEAD_DIM
SSD_GROUPS = 2
SSD_HPG = SSD_HEADS // SSD_GROUPS
SSD_STATE = 128
SSD_CONV = 5
SSD_CHUNK = 128
CONV_CH = SSD_INNER + 2 * SSD_GROUPS * SSD_STATE
POOL_WINDOWS = (2, 4, 8, 16)
POOL_GROUP = 128
POOL_WIDTH = POOL_GROUP * len(POOL_WINDOWS)
MIX_WIDTH = ATT_OUT + SSD_INNER + POOL_WIDTH
D_FF = 5632
RMS_EPS = 1e-6
NEG_INF = -1e30

LANES = 128
IN_TILE = 768
IN_TILES_PER_GROUP = 2
N_IN_TILES = 5 * IN_TILES_PER_GROUP
IN_TM = 1024
VMEM_LIMIT = 56 * 1024 * 1024
OUT_TM = 512
FFN_TM = 1024
FFN_TF = 512


def _cparams(sem):
    return pltpu.CompilerParams(dimension_semantics=sem, vmem_limit_bytes=VMEM_LIMIT)


def _silu(x):
    return x * (1.0 / (1.0 + jnp.exp(-x)))


def _softplus(x):
    return jnp.maximum(x, 0.0) + jnp.log(1.0 + jnp.exp(-jnp.abs(x)))


def _inproj_kernel(x_ref, nw_ref, w_ref, wdt_ref, q0_ref, q1_ref, q2_ref, zux_ref, dt_ref,
                   h_ref, stage_ref, *, tm):
    j = pl.program_id(1)
    jg = j // IN_TILES_PER_GROUP

    @pl.when(j == 0)
    def _():
        x = x_ref[...]
        ms = jnp.mean(x * x, axis=-1, keepdims=True)
        h = x * lax.rsqrt(ms + RMS_EPS) * nw_ref[...]
        h_ref[...] = h.astype(BF16)
        dt_ref[...] = jnp.dot(h_ref[...], wdt_ref[...], preferred_element_type=F32)

    def proj():
        return jnp.dot(h_ref[...], w_ref[...], preferred_element_type=F32)

    @pl.when(jg == 0)
    def _():
        q0_ref[0, 0] = proj().astype(BF16)

    for g, (d, out_ref) in enumerate(zip(DILATIONS, (q0_ref, q1_ref, q2_ref))):
        if g == 0:
            continue

        @pl.when(jg == g)
        def _(d=d, out_ref=out_ref):
            res = proj()
            for cb in range(IN_TILE // LANES):
                stage_ref[cb] = res[:, cb * LANES:(cb + 1) * LANES]
            for r in range(d):
                for cb in range(IN_TILE // LANES):
                    out_ref[0, r, :, cb * LANES:(cb + 1) * LANES] = (
                        stage_ref[cb, pl.ds(r, tm // d, stride=d), :].astype(BF16))

    @pl.when(jg >= N_DIL)
    def _():
        zux_ref[...] = proj()


def _inproj(x2, nw, wcat, wdt, *, batch, seq, tm):
    t = x2.shape[0]
    per_b = seq // tm
    grid = (t // tm, N_IN_TILES)
    tpg = IN_TILES_PER_GROUP

    def qspec(g, d):
        return pl.BlockSpec((1, d, tm // d, IN_TILE),
                            lambda i, j: (i // per_b, 0, i % per_b, jnp.clip(j - g * tpg, 0, tpg - 1)))

    out_shape = [jax.ShapeDtypeStruct((batch, d, seq // d, tpg * IN_TILE), BF16) for d in DILATIONS]
    out_shape += [jax.ShapeDtypeStruct((t, 2 * tpg * IN_TILE), F32), jax.ShapeDtypeStruct((t, LANES), F32)]
    return pl.pallas_call(
        functools.partial(_inproj_kernel, tm=tm),
        grid=grid,
        in_specs=[
            pl.BlockSpec((tm, D_MODEL), lambda i, j: (i, 0)),
            pl.BlockSpec((1, D_MODEL), lambda i, j: (0, 0)),
            pl.BlockSpec((D_MODEL, IN_TILE), lambda i, j: (0, j)),
            pl.BlockSpec((D_MODEL, LANES), lambda i, j: (0, 0)),
        ],
        out_specs=[qspec(g, d) for g, d in enumerate(DILATIONS)] + [
            pl.BlockSpec((tm, IN_TILE), lambda i, j: (i, jnp.maximum(j - N_DIL * tpg, 0))),
            pl.BlockSpec((tm, LANES), lambda i, j: (i, 0)),
        ],
        out_shape=out_shape,
        scratch_shapes=[pltpu.VMEM((tm, D_MODEL), BF16), pltpu.VMEM((IN_TILE // LANES, tm, LANES), F32)],
        compiler_params=_cparams(("arbitrary", "arbitrary")),
        name="inproj",
    )(x2, nw, wcat, wdt)


ATT_QB = 128
ATT_KW = 256
ATT_UNROLL = 32


def _attn_kernel(slopes_ref, q_ref, k_ref, v_ref, o_ref, lse_ref, bias_ref, *, seq_len, dil):
    sp = pl.program_id(0)
    is_a = lax.broadcasted_iota(jnp.int32, (ATT_QB, LANES), 1) < ATT_HEAD_DIM
    n_blocks = seq_len // ATT_QB

    @pl.when(pl.program_id(1) == 0)
    def _():
        slope_a = slopes_ref[2 * sp] * float(dil)
        slope_b = slopes_ref[2 * sp + 1] * float(dil)
        rows = lax.broadcasted_iota(jnp.int32, (2 * ATT_QB, ATT_KW), 0)
        cols = lax.broadcasted_iota(jnp.int32, (2 * ATT_QB, ATT_KW), 1)
        rel_base = cols - (rows & (ATT_QB - 1))
        slope = jnp.where(rows < ATT_QB, slope_a, slope_b)
        for t in range(3):
            dist = jnp.abs(rel_base - t * ATT_HALF)
            bias_ref[t] = jnp.where(dist <= ATT_HALF, -slope * dist.astype(F32), NEG_INF)

    def body(it, carry):
        r = lax.shift_right_logical(it, n_blocks.bit_length() - 1)
        i = it & (n_blocks - 1)
        q0 = pl.multiple_of(i * ATT_QB, ATT_QB)
        k0 = pl.multiple_of(jnp.clip(i * ATT_QB - ATT_HALF, 0, seq_len - ATT_KW), ATT_HALF)
        q2 = q_ref[0, r, pl.ds(q0, ATT_QB), :] * jnp.asarray(ATT_HEAD_DIM ** -0.5, BF16)
        zero = jnp.zeros_like(q2)
        qs = jnp.concatenate([jnp.where(is_a, q2, zero), jnp.where(is_a, zero, q2)], axis=0)
        kk = k_ref[0, r, pl.ds(k0, ATT_KW), :]
        vv = v_ref[0, r, pl.ds(k0, ATT_KW), :]
        s = lax.dot_general(qs, kk, (((1,), (1,)), ((), ())), preferred_element_type=F32)
        s = s + bias_ref[lax.shift_right_logical(q0 - k0, ATT_HALF.bit_length() - 1)]
        m = jnp.max(s, axis=-1, keepdims=True)
        p = jnp.exp(s - m)
        l = jnp.sum(p, axis=-1, keepdims=True)
        pv = jnp.dot(p.astype(BF16), vv, preferred_element_type=F32)
        o = pv * (1.0 / l)
        lse = jnp.broadcast_to(m + jnp.log(l), (2 * ATT_QB, LANES))
        o_ref[0, r, pl.ds(q0, ATT_QB), :] = jnp.where(is_a, o[:ATT_QB], o[ATT_QB:]).astype(BF16)
        lse_ref[0, r, pl.ds(q0, ATT_QB), :] = jnp.where(is_a, lse[:ATT_QB], lse[ATT_QB:])
        return carry

    lax.fori_loop(0, dil * n_blocks, body, 0, unroll=ATT_UNROLL)


def _attention_group(qkv, slopes, *, dil):
    batch, d, seq_len, _ = qkv.shape
    n_pairs = ATT_OUT // LANES

    def spec(col0):
        return pl.BlockSpec((1, d, seq_len, LANES), lambda sp, b: (b, 0, 0, col0 + sp))

    shape = (batch, d, seq_len, ATT_OUT)
    return pl.pallas_call(
        functools.partial(_attn_kernel, seq_len=seq_len, dil=dil),
        grid=(n_pairs, batch),
        in_specs=[pl.BlockSpec(memory_space=pltpu.SMEM), spec(0), spec(n_pairs), spec(2 * n_pairs)],
        out_specs=[spec(0), spec(0)],
        out_shape=[jax.ShapeDtypeStruct(shape, BF16), jax.ShapeDtypeStruct(shape, F32)],
        scratch_shapes=[pltpu.VMEM((3, 2 * ATT_QB, ATT_KW), F32)],
        compiler_params=_cparams(("arbitrary", "arbitrary")),
        name=f"attn_d{dil}",
    )(slopes, qkv, qkv, qkv)


def _blend_groups(ol_refs, so_ref, sl_ref, tm):
    (o0_ref, l0_ref), rest = ol_refs[0], ol_refs[1:]
    outs = [o0_ref[0, 0].astype(F32)]
    lses = [l0_ref[0, 0]]
    for d, (o_ref, l_ref) in zip(DILATIONS[1:], rest):
        for r in range(d):
            for cb in range(ATT_OUT // LANES):
                cols = slice(cb * LANES, (cb + 1) * LANES)
                so_ref[cb, pl.ds(r, tm // d, stride=d), :] = o_ref[0, r, :, cols].astype(F32)
                sl_ref[cb, pl.ds(r, tm // d, stride=d), :] = l_ref[0, r, :, cols]
        outs.append(jnp.concatenate([so_ref[cb] for cb in range(ATT_OUT // LANES)], axis=1))
        lses.append(jnp.concatenate([sl_ref[cb] for cb in range(ATT_OUT // LANES)], axis=1))
    mx = jnp.maximum(jnp.maximum(lses[0], lses[1]), lses[2])
    ws = [jnp.exp(l - mx) for l in lses]
    num = ws[0] * outs[0] + ws[1] * outs[1] + ws[2] * outs[2]
    return (num / (ws[0] + ws[1] + ws[2])).astype(BF16)


HALO = 8
SSD_SUB = 2
SSD_ROWS = SSD_SUB * SSD_CHUNK


def _expand_rows(fac, col0, is_a):
    parts = []
    for j in range(SSD_HPG // 2):
        a = fac[:, col0 + 2 * j:col0 + 2 * j + 1]
        b = fac[:, col0 + 2 * j + 1:col0 + 2 * j + 2]
        parts.append(jnp.where(is_a, a, b))
    return jnp.concatenate(parts, axis=1)


def _split2(v):
    hi = v.astype(BF16)
    lo = (v - hi.astype(F32)).astype(BF16)
    return jnp.concatenate([hi, lo], axis=1)


def _split3(v):
    hi = v.astype(BF16)
    r1 = v - hi.astype(F32)
    mid = r1.astype(BF16)
    lo = (r1 - mid.astype(F32)).astype(BF16)
    return jnp.concatenate([hi, mid, lo], axis=1)


def _ssd_kernel(z_ref, xbc_ref, prev_ref, next_ref, dt_ref, cw_ref, cb_ref, dtb_ref, alog_ref,
                dskip_ref, nw_ref, out_ref, ext_ref, xc_ref, yf_ref, st_ref, sel_ref, *, n_blocks):
    q = SSD_CHUNK
    p = pl.program_id(1)
    c = pl.program_id(2)
    fwd = p == 0
    cc = jnp.where(fwd, c, n_blocks - 1 - c)
    row0 = pl.multiple_of(cc * SSD_ROWS, SSD_ROWS)

    @pl.when(jnp.logical_and(fwd, c == 0))
    def _():
        yf_ref[...] = jnp.zeros_like(yf_ref)

    @pl.when(c == 0)
    def _():
        st_ref[...] = jnp.zeros_like(st_ref)
        src = lax.broadcasted_iota(jnp.int32, (2 * LANES, SSD_INNER), 0) & (LANES - 1)
        dst = lax.shift_right_logical(lax.broadcasted_iota(jnp.int32, (2 * LANES, SSD_INNER), 1),
                                      SSD_HEAD_DIM.bit_length() - 1)
        sel_ref[...] = jnp.where(src == dst, 1.0, 0.0).astype(BF16)

    @pl.when(fwd)
    def _():
        for cb in range(CONV_CH // LANES):
            cols = slice(cb * LANES, (cb + 1) * LANES)
            ext_ref[cb, 0:HALO, :] = jnp.where(cc == 0, 0.0, prev_ref[:, cols])
            ext_ref[cb, HALO:HALO + SSD_ROWS, :] = xbc_ref[:, cols]
            ext_ref[cb, HALO + SSD_ROWS:, :] = jnp.where(cc == n_blocks - 1, 0.0, next_ref[:, cols])
            acc = jnp.broadcast_to(cb_ref[:, cols], (SSD_ROWS, LANES))
            for k in range(SSD_CONV):
                acc = acc + cw_ref[k:k + 1, cols] * ext_ref[cb, pl.ds(HALO - SSD_CONV // 2 + k, SSD_ROWS), :]
            xc_ref[pl.ds(row0, SSD_ROWS), cols] = _silu(acc)

    gn = SSD_GROUPS * SSD_STATE
    half = SSD_INNER // SSD_GROUPS
    ri = lax.broadcasted_iota(jnp.int32, (q, q), 0)
    ci = lax.broadcasted_iota(jnp.int32, (q, q), 1)
    before = (ci - ri) * jnp.where(fwd, 1, -1) <= 0
    cum_lhs = jnp.concatenate([before.astype(BF16), jnp.ones((q, q), BF16)], axis=0)
    is_a = lax.broadcasted_iota(jnp.int32, (q, LANES), 1) < SSD_HEAD_DIM
    neg_a = -jnp.exp(alog_ref[...])

    for k in range(SSD_SUB):
        off = pl.multiple_of(jnp.where(fwd, k, SSD_SUB - 1 - k) * q, q)
        rows = pl.ds(row0 + off, q)
        xs = xc_ref[rows, 0:SSD_INNER]

        dt_all = _softplus(dt_ref[pl.ds(off, q), :] + dtb_ref[...])
        dta_all = dt_all * neg_a
        dt_d = jnp.where(fwd, dt_all, pltpu.roll(dt_all, LANES - SSD_HEADS, 1))
        dta_d = jnp.where(fwd, dta_all, pltpu.roll(dta_all, LANES - SSD_HEADS, 1))
        sums = jnp.dot(cum_lhs, _split3(dta_d), preferred_element_type=F32)
        sums = sums[:, 0:LANES] + sums[:, LANES:2 * LANES] + sums[:, 2 * LANES:]
        cum = sums[:q]
        total = sums[q:]
        wide = jnp.dot(jnp.concatenate([_split2(jnp.exp(cum)), _split2(jnp.exp(total - cum) * dt_d)], axis=0),
                       sel_ref[...], preferred_element_type=F32)
        e_cum_w = wide[:q]
        d_state_w = wide[q:]
        d_chunk = jnp.exp(total[0:8])
        log_dt_t = jnp.log(dt_d).T
        shifted_t = cum.T - log_dt_t

        ys = []
        for g in range(SSD_GROUPS):
            col0 = g * SSD_HPG
            gcols = slice(g * half, (g + 1) * half)
            xg = xs[:, gcols]
            bg = xc_ref[rows, SSD_INNER + g * SSD_STATE:SSD_INNER + (g + 1) * SSD_STATE]
            cg = xc_ref[rows, SSD_INNER + gn + g * SSD_STATE:SSD_INNER + gn + (g + 1) * SSD_STATE]
            xg16 = xg.astype(BF16)
            bg16 = bg.astype(BF16)
            cg16 = cg.astype(BF16)
            cbm = lax.dot_general(cg16, bg16, (((1,), (1,)), ((), ())), preferred_element_type=F32)
            cbm = jnp.where(before, cbm, 0.0)
            y_parts = []
            for j in range(SSD_HPG // 2):
                ms = []
                for h in (2 * j, 2 * j + 1):
                    col = col0 + h
                    diff = cum[:, col:col + 1] - shifted_t[col:col + 1, :]
                    ms.append((cbm * jnp.exp(jnp.minimum(diff, log_dt_t[col:col + 1, :]))).astype(BF16))
                xp = xg16[:, j * LANES:(j + 1) * LANES]
                zero = jnp.zeros_like(xp)
                xdiag = jnp.concatenate([jnp.where(is_a, xp, zero), jnp.where(is_a, zero, xp)], axis=0)
                y_parts.append(jnp.dot(jnp.concatenate(ms, axis=1), xdiag, preferred_element_type=F32))
            y_diag = jnp.concatenate(y_parts, axis=1)
            st = st_ref[g]
            y_off = jnp.dot(cg16, st.astype(BF16), preferred_element_type=F32) * e_cum_w[:, gcols]
            xd = (xg * d_state_w[:, gcols]).astype(BF16)
            new = jnp.dot(bg.T.astype(BF16), xd, preferred_element_type=F32)
            keep = _expand_rows(d_chunk, col0, is_a[0:8])
            st_ref[g] = (st.reshape(SSD_STATE // 8, 8, -1) * keep[None]).reshape(st.shape) + new
            ys.append(y_diag + y_off)
        y = jnp.concatenate(ys, axis=1)
        yf_ref[rows, :] += y

    @pl.when(jnp.logical_not(fwd))
    def _():
        rows = pl.ds(row0, SSD_ROWS)
        yt = (yf_ref[rows, :] + xc_ref[rows, 0:SSD_INNER] * dskip_ref[...]) * _silu(z_ref[...])
        for g in range(SSD_GROUPS):
            v = yt[:, g * half:(g + 1) * half]
            ms = jnp.mean(v * v, axis=-1, keepdims=True)
            out_ref[:, g * half:(g + 1) * half] = (
                v * lax.rsqrt(ms + RMS_EPS) * nw_ref[:, g * half:(g + 1) * half]).astype(BF16)


def _ssd(zux, dt, cw, cb, dtb, alog, dskip, nw, *, batch, seq):
    t = batch * seq
    rows = SSD_ROWS
    n_blocks = seq // rows
    hb = rows // HALO

    def row_blk(b, p, c):
        return b * n_blocks + jnp.where(p == 0, c, n_blocks - 1 - c)

    def conv_blk(b, p, c):
        return b * n_blocks + jnp.where(p == 0, c, n_blocks - 1)

    def prev_blk(b, p, c):
        return jnp.maximum(conv_blk(b, p, c) * hb - 1, b * n_blocks * hb)

    def next_blk(b, p, c):
        return jnp.minimum((conv_blk(b, p, c) + 1) * hb, (b + 1) * n_blocks * hb - 1)

    def z_blk(b, p, c):
        return jnp.where(p == 0, b * n_blocks + n_blocks - 1, row_blk(b, p, c))

    def const(shape):
        return pl.BlockSpec(shape, lambda b, p, c: (0, 0))

    return pl.pallas_call(
        functools.partial(_ssd_kernel, n_blocks=n_blocks),
        grid=(batch, 2, n_blocks),
        in_specs=[
            pl.BlockSpec((rows, SSD_INNER), lambda b, p, c: (z_blk(b, p, c), 0)),
            pl.BlockSpec((rows, CONV_CH), lambda b, p, c: (conv_blk(b, p, c), 1)),
            pl.BlockSpec((HALO, CONV_CH), lambda b, p, c: (prev_blk(b, p, c), 1)),
            pl.BlockSpec((HALO, CONV_CH), lambda b, p, c: (next_blk(b, p, c), 1)),
            pl.BlockSpec((rows, LANES), lambda b, p, c: (row_blk(b, p, c), 0)),
            const((8, CONV_CH)), const((1, CONV_CH)), const((1, LANES)), const((1, LANES)),
            const((1, SSD_INNER)), const((1, SSD_INNER)),
        ],
        out_specs=pl.BlockSpec((rows, SSD_INNER), lambda b, p, c: (z_blk(b, p, c), 0)),
        out_shape=jax.ShapeDtypeStruct((t, SSD_INNER), BF16),
        scratch_shapes=[
            pltpu.VMEM((CONV_CH // LANES, rows + 2 * HALO, LANES), F32),
            pltpu.VMEM((seq, CONV_CH), F32),
            pltpu.VMEM((seq, SSD_INNER), F32),
            pltpu.VMEM((SSD_GROUPS, SSD_STATE, SSD_INNER // SSD_GROUPS), F32),
            pltpu.VMEM((2 * LANES, SSD_INNER), BF16),
        ],
        compiler_params=_cparams(("arbitrary", "arbitrary", "arbitrary")),
        name="ssd",
    )(zux, zux, zux, zux, dt, cw, cb, dtb, alog, dskip, nw)


POOL_PAD = 16
POOL_ROWS = 512


def _pool_kernel(u_ref, w_ref, sc_ref, out_ref, pad_ref, *, seq):
    zeros = jnp.zeros((POOL_PAD, POOL_GROUP), F32)
    for g in range(len(POOL_WINDOWS)):
        pad_ref[g, 0:POOL_PAD, :] = zeros
        pad_ref[g, POOL_PAD + seq:, :] = zeros
        pad_ref[g, POOL_PAD:POOL_PAD + seq, :] = u_ref[:, g * POOL_GROUP:(g + 1) * POOL_GROUP]
    for r0 in range(0, seq, POOL_ROWS):
        pos = r0 + lax.broadcasted_iota(jnp.int32, (POOL_ROWS, POOL_GROUP), 0)
        for g, w in enumerate(POOL_WINDOWS):
            cols = slice(g * POOL_GROUP, (g + 1) * POOL_GROUP)
            u = pad_ref[g, POOL_PAD + r0:POOL_PAD + r0 + POOL_ROWS, :]
            s = None
            for j in range(-(w // 2), w // 2):
                term = pad_ref[g, POOL_PAD + r0 + j:POOL_PAD + r0 + j + POOL_ROWS, :]
                s = term if s is None else s + term
            cnt = jnp.minimum(pos + w // 2, seq) - jnp.maximum(pos - w // 2, 0)
            mean = s / cnt.astype(F32)
            y = jnp.dot((mean - u).astype(BF16), w_ref[g], preferred_element_type=F32)
            out_ref[r0:r0 + POOL_ROWS, cols] = (y * sc_ref[:, cols]).astype(BF16)


def _pool(zux, pw, psc, *, batch, seq):
    t = batch * seq
    return pl.pallas_call(
        functools.partial(_pool_kernel, seq=seq),
        scratch_shapes=[pltpu.VMEM((len(POOL_WINDOWS), seq + 2 * POOL_PAD, POOL_GROUP), F32)],
        grid=(batch,),
        in_specs=[
            pl.BlockSpec((seq, POOL_WIDTH), lambda b: (b, SSD_INNER // POOL_WIDTH)),
            pl.BlockSpec((len(POOL_WINDOWS), POOL_GROUP, POOL_GROUP), lambda b: (0, 0, 0)),
            pl.BlockSpec((1, POOL_WIDTH), lambda b: (0, 0)),
        ],
        out_specs=pl.BlockSpec((seq, POOL_WIDTH), lambda b: (b, 0)),
        out_shape=jax.ShapeDtypeStruct((t, POOL_WIDTH), BF16),
        compiler_params=_cparams(("arbitrary",)),
        name="pool",
    )(zux, pw, psc)


def _outproj_kernel(x_ref, o0_ref, l0_ref, o1_ref, l1_ref, o2_ref, l2_ref, s_ref, p_ref, w_ref, o_ref,
                    so_ref, sl_ref, *, tm):
    s0, p0 = ATT_OUT, ATT_OUT + SSD_INNER
    att = _blend_groups(((o0_ref, l0_ref), (o1_ref, l1_ref), (o2_ref, l2_ref)), so_ref, sl_ref, tm)
    acc = jnp.dot(att, w_ref[0, 0:s0, :], preferred_element_type=F32)
    acc = acc + jnp.dot(s_ref[...], w_ref[0, s0:p0, :], preferred_element_type=F32)
    acc = acc + jnp.dot(p_ref[...], w_ref[0, p0:, :], preferred_element_type=F32)
    o_ref[...] = x_ref[...] + acc


def _outproj(x2, ol, ssd, pool, wo, layer, *, batch, seq, tm):
    t = x2.shape[0]
    per_b = seq // tm
    in_specs = [pl.BlockSpec((tm, D_MODEL), lambda i: (i, 0))]
    args = [x2]
    for d, (o, l) in zip(DILATIONS, ol):
        spec = pl.BlockSpec((1, d, tm // d, ATT_OUT), lambda i: (i // per_b, 0, i % per_b, 0))
        in_specs += [spec, spec]
        args += [o, l]
    in_specs += [
        pl.BlockSpec((tm, SSD_INNER), lambda i: (i, 0)),
        pl.BlockSpec((tm, POOL_WIDTH), lambda i: (i, 0)),
        pl.BlockSpec((1, MIX_WIDTH, D_MODEL), lambda i: (layer, 0, 0)),
    ]
    return pl.pallas_call(
        functools.partial(_outproj_kernel, tm=tm),
        grid=(t // tm,),
        in_specs=in_specs,
        out_specs=pl.BlockSpec((tm, D_MODEL), lambda i: (i, 0)),
        out_shape=jax.ShapeDtypeStruct((t, D_MODEL), F32),
        scratch_shapes=[pltpu.VMEM((ATT_OUT // LANES, tm, LANES), F32)] * 2,
        compiler_params=_cparams(("arbitrary",)),
        name="outproj",
    )(*args, ssd, pool, wo)


def _ffn_kernel(x_ref, nw_ref, wg_ref, wu_ref, wd_ref, fw_ref, o_ref, h_ref, *, n_ff, final):
    j = pl.program_id(1)

    @pl.when(j == 0)
    def _():
        x = x_ref[...]
        ms = jnp.mean(x * x, axis=-1, keepdims=True)
        h_ref[...] = (x * lax.rsqrt(ms + RMS_EPS) * nw_ref[...]).astype(BF16)
        o_ref[...] = x

    h = h_ref[...]
    gate = jnp.dot(h, wg_ref[0], preferred_element_type=F32)
    up = jnp.dot(h, wu_ref[0], preferred_element_type=F32)
    act = (_silu(gate) * up).astype(BF16)
    o_ref[...] += jnp.dot(act, wd_ref[0], preferred_element_type=F32)

    if final:
        @pl.when(j == n_ff - 1)
        def _():
            y = o_ref[...]
            ms = jnp.mean(y * y, axis=-1, keepdims=True)
            o_ref[...] = y * lax.rsqrt(ms + RMS_EPS) * fw_ref[...]


def _ffn(x2, nw, wg, wu, wd, fw, layer, *, tm, tf, final):
    t = x2.shape[0]
    n_ff = D_FF // tf
    return pl.pallas_call(
        functools.partial(_ffn_kernel, n_ff=n_ff, final=final),
        grid=(t // tm, n_ff),
        in_specs=[
            pl.BlockSpec((tm, D_MODEL), lambda i, j: (i, 0)),
            pl.BlockSpec((1, D_MODEL), lambda i, j: (0, 0)),
            pl.BlockSpec((1, D_MODEL, tf), lambda i, j: (layer, 0, j)),
            pl.BlockSpec((1, D_MODEL, tf), lambda i, j: (layer, 0, j)),
            pl.BlockSpec((1, tf, D_MODEL), lambda i, j: (layer, j, 0)),
            pl.BlockSpec((1, D_MODEL), lambda i, j: (0, 0)),
        ],
        out_specs=pl.BlockSpec((tm, D_MODEL), lambda i, j: (i, 0)),
        out_shape=jax.ShapeDtypeStruct((t, D_MODEL), F32),
        scratch_shapes=[pltpu.VMEM((tm, D_MODEL), BF16)],
        compiler_params=_cparams(("arbitrary", "arbitrary")),
        name="ffn_final" if final else "ffn",
    )(x2, nw, wg, wu, wd, fw)


PACK_W = 512
PACK_U_TILE = 11


def _pack_kernel(wt_ref, tail_ref, o_ref, wdt_ref):
    j = pl.program_id(0)
    n_dt = 2 * SSD_HEADS

    @pl.when(j == 0)
    def _():
        lane = lax.broadcasted_iota(jnp.int32, (D_MODEL, LANES), 1)
        wdt_ref[...] = jnp.where(lane < n_dt, tail_ref[:, 0:LANES], 0.0).astype(BF16)

    @pl.when(j != PACK_U_TILE)
    def _():
        o_ref[...] = wt_ref[0].T.astype(BF16)

    @pl.when(j == PACK_U_TILE)
    def _():
        o_ref[...] = tail_ref[:, n_dt:n_dt + POOL_WIDTH].astype(BF16)


def _pack_in_weights(w_in, w_in_t, layer):
    z0 = 3 * ATT_WIDTH
    dt0 = z0 + SSD_INNER + CONV_CH
    n_tail = w_in.shape[2] - dt0
    n_qkv = z0 // PACK_W

    def src_tile(j):
        qkv = (j % N_DIL) * 3 + j // N_DIL
        return jnp.where(j < n_qkv, qkv, jnp.where(j < PACK_U_TILE, j, j - 1))

    return pl.pallas_call(
        _pack_kernel,
        grid=(N_IN_TILES * IN_TILE // PACK_W,),
        in_specs=[
            pl.BlockSpec((1, PACK_W, D_MODEL), lambda j: (layer, src_tile(j), 0)),
            pl.BlockSpec((D_MODEL, n_tail), lambda j: (0, 0)),
        ],
        out_specs=[pl.BlockSpec((D_MODEL, PACK_W), lambda j: (0, j)),
                   pl.BlockSpec((D_MODEL, LANES), lambda j: (0, 0))],
        out_shape=[jax.ShapeDtypeStruct((D_MODEL, N_IN_TILES * IN_TILE), BF16),
                   jax.ShapeDtypeStruct((D_MODEL, LANES), BF16)],
        compiler_params=_cparams(("arbitrary",)),
        name="pack_w_in",
    )(w_in_t, w_in[layer, :, dt0:])


def _pad_lanes(v):
    return jnp.pad(v.reshape(1, -1).astype(F32), ((0, 0), (0, LANES - v.size)))


def _layer(x2, slopes, p, fw, *, batch, seq, final):
    wcat, wdt = _pack_in_weights(p["w_in"], p["w_in_t"], p["layer"])
    q0, q1, q2, zux, dt = _inproj(x2, p["norm1_w"].reshape(1, -1), wcat, wdt,
                                  batch=batch, seq=seq, tm=IN_TM)
    ol = [_attention_group(qkv, slopes[g], dil=d) for g, (d, qkv) in enumerate(zip(DILATIONS, (q0, q1, q2)))]
    cw = jnp.pad(p["conv_w"].astype(F32), ((0, 8 - SSD_CONV), (0, 0)))
    ssd = _ssd(zux, dt, cw, p["conv_b"].reshape(1, -1).astype(F32), _pad_lanes(p["dt_bias"]),
               _pad_lanes(p["a_log"]), jnp.repeat(p["d_skip"].astype(F32), SSD_HEAD_DIM).reshape(1, -1),
               p["ssd_norm_w"].reshape(1, -1).astype(F32), batch=batch, seq=seq)
    pool = _pool(zux, p["pool_w"].astype(BF16), p["pool_scale"].reshape(1, -1).astype(F32),
                 batch=batch, seq=seq)
    x2 = _outproj(x2, ol, ssd, pool, p["w_out"], p["layer"], batch=batch, seq=seq, tm=OUT_TM)
    return _ffn(x2, p["norm2_w"].reshape(1, -1), p["w_gate"], p["w_up"], p["w_down"], fw.reshape(1, -1),
                p["layer"], tm=FFN_TM, tf=FFN_TF, final=final)


def kernel(x, norm1_w, w_in, conv_w, conv_b, dt_bias, a_log, d_skip, ssd_norm_w, pool_w, pool_scale,
           w_out, norm2_w, w_gate, w_up, w_down, final_norm_w):
    batch, seq, _ = x.shape
    depth = w_in.shape[0]
    k = jnp.arange(1, ATT_HEADS + 1, dtype=F32)
    slopes = (2.0 ** (-8.0 * k / ATT_HEADS)).reshape(N_DIL, ATT_SLOTS)
    x2 = x.reshape(batch * seq, D_MODEL)
    wo, wg, wu, wd = (w.astype(BF16) for w in (w_out, w_gate, w_up, w_down))
    w_in_t = jnp.swapaxes(w_in, 1, 2)
    for l in range(depth):
        p = dict(norm1_w=norm1_w[l], w_in=w_in, w_in_t=w_in_t, layer=l, conv_w=conv_w[l], conv_b=conv_b[l], dt_bias=dt_bias[l],
                 a_log=a_log[l], d_skip=d_skip[l], ssd_norm_w=ssd_norm_w[l], pool_w=pool_w[l],
                 pool_scale=pool_scale[l], w_out=wo, norm2_w=norm2_w[l], w_gate=wg, w_up=wu, w_down=wd)
        x2 = _layer(x2, slopes, p, final_norm_w, batch=batch, seq=seq, final=(l == depth - 1))
    return x2.reshape(batch, seq, D_MODEL)
```

```python
import functools
import math

import jax
import jax.numpy as jnp
from jax import lax
from jax.experimental import pallas as pl
from jax.experimental.pallas import tpu as pltpu

F32 = jnp.float32
BF16 = jnp.bfloat16

D_MODEL = 2048
ATT_HEAD_DIM = 64
ATT_SLOTS = 8
DILATIONS = (1, 4, 16)
ATT_HALF = 64
N_DIL = len(DILATIONS)
ATT_HEADS = ATT_SLOTS * N_DIL
ATT_WIDTH = ATT_HEADS * ATT_HEAD_DIM
ATT_OUT = ATT_SLOTS * ATT_HEAD_DIM
SSD_HEAD_DIM = 64
SSD_INNER = D_MODEL // 2
SSD_HEADS = SSD_INNER // SSD_HEAD_DIM
SSD_GROUPS = 2
SSD_HPG = SSD_HEADS // SSD_GROUPS
SSD_STATE = 128
SSD_CONV = 5
SSD_CHUNK = 128
CONV_CH = SSD_INNER + 2 * SSD_GROUPS * SSD_STATE
POOL_WINDOWS = (2, 4, 8, 16)
POOL_GROUP = 128
POOL_WIDTH = POOL_GROUP * len(POOL_WINDOWS)
MIX_WIDTH = ATT_OUT + SSD_INNER + POOL_WIDTH
D_FF = 5632
RMS_EPS = 1e-6
NEG_INF = -1e30

LANES = 128
IN_TILE = 768
IN_TILES_PER_GROUP = 2
N_IN_TILES = 5 * IN_TILES_PER_GROUP
IN_TM = 1024
VMEM_LIMIT = 56 * 1024 * 1024
OUT_TM = 512
FFN_TM = 1024
FFN_TF = 512


def _cparams(sem):
    return pltpu.CompilerParams(dimension_semantics=sem, vmem_limit_bytes=VMEM_LIMIT)


def _silu(x):
    half = 0.5 * x
    return half + half * jnp.tanh(half)


def _softplus(x):
    return jnp.maximum(x, 0.0) + jnp.log(1.0 + jnp.exp(-jnp.abs(x)))


def _inproj_kernel(x_ref, nw_ref, w_ref, wdt_ref, q0_ref, q1_ref, q2_ref, zux_ref, dt_ref,
                   h_ref, stage_ref, *, tm):
    j = pl.program_id(1)
    jg = j // IN_TILES_PER_GROUP

    @pl.when(j == 0)
    def _():
        x = x_ref[...]
        ms = jnp.mean(x * x, axis=-1, keepdims=True)
        h = x * lax.rsqrt(ms + RMS_EPS) * nw_ref[...]
        h_ref[...] = h.astype(BF16)
        dt_ref[...] = jnp.dot(h_ref[...], wdt_ref[...], preferred_element_type=F32)

    def proj():
        return jnp.dot(h_ref[...], w_ref[...], preferred_element_type=F32)

    @pl.when(jg == 0)
    def _():
        q0_ref[0, 0] = proj().astype(BF16)

    for g, (d, out_ref) in enumerate(zip(DILATIONS, (q0_ref, q1_ref, q2_ref))):
        if g == 0:
            continue

        @pl.when(jg == g)
        def _(d=d, out_ref=out_ref):
            res = proj()
            for cb in range(IN_TILE // LANES):
                stage_ref[cb] = res[:, cb * LANES:(cb + 1) * LANES]
            for r in range(d):
                for cb in range(IN_TILE // LANES):
                    out_ref[0, r, :, cb * LANES:(cb + 1) * LANES] = (
                        stage_ref[cb, pl.ds(r, tm // d, stride=d), :].astype(BF16))

    @pl.when(jg >= N_DIL)
    def _():
        zux_ref[...] = proj()


def _inproj(x2, nw, wcat, wdt, *, batch, seq, tm):
    t = x2.shape[0]
    per_b = seq // tm
    grid = (t // tm, N_IN_TILES)
    tpg = IN_TILES_PER_GROUP

    def qspec(g, d):
        return pl.BlockSpec((1, d, tm // d, IN_TILE),
                            lambda i, j: (i // per_b, 0, i % per_b, jnp.clip(j - g * tpg, 0, tpg - 1)))

    out_shape = [jax.ShapeDtypeStruct((batch, d, seq // d, tpg * IN_TILE), BF16) for d in DILATIONS]
    out_shape += [jax.ShapeDtypeStruct((t, 2 * tpg * IN_TILE), F32), jax.ShapeDtypeStruct((t, LANES), F32)]
    return pl.pallas_call(
        functools.partial(_inproj_kernel, tm=tm),
        grid=grid,
        in_specs=[
            pl.BlockSpec((tm, D_MODEL), lambda i, j: (i, 0)),
            pl.BlockSpec((1, D_MODEL), lambda i, j: (0, 0)),
            pl.BlockSpec((D_MODEL, IN_TILE), lambda i, j: (0, j)),
            pl.BlockSpec((D_MODEL, LANES), lambda i, j: (0, 0)),
        ],
        out_specs=[qspec(g, d) for g, d in enumerate(DILATIONS)] + [
            pl.BlockSpec((tm, IN_TILE), lambda i, j: (i, jnp.maximum(j - N_DIL * tpg, 0))),
            pl.BlockSpec((tm, LANES), lambda i, j: (i, 0)),
        ],
        out_shape=out_shape,
        scratch_shapes=[pltpu.VMEM((tm, D_MODEL), BF16), pltpu.VMEM((IN_TILE // LANES, tm, LANES), F32)],
        compiler_params=_cparams(("arbitrary", "arbitrary")),
        name="inproj",
    )(x2, nw, wcat, wdt)


ATT_QB = 128
ATT_KW = 256
ATT_UNROLL = 32


def _attn_kernel(slopes_ref, q_ref, k_ref, v_ref, o_ref, lse_ref, bias_ref, *, seq_len, dil):
    sp = pl.program_id(0)
    is_a = lax.broadcasted_iota(jnp.int32, (ATT_QB, LANES), 1) < ATT_HEAD_DIM
    n_blocks = seq_len // ATT_QB

    @pl.when(pl.program_id(1) == 0)
    def _():
        slope_a = slopes_ref[2 * sp] * float(dil)
        slope_b = slopes_ref[2 * sp + 1] * float(dil)
        rows = lax.broadcasted_iota(jnp.int32, (2 * ATT_QB, ATT_KW), 0)
        cols = lax.broadcasted_iota(jnp.int32, (2 * ATT_QB, ATT_KW), 1)
        rel_base = cols - (rows & (ATT_QB - 1))
        slope = jnp.where(rows < ATT_QB, slope_a, slope_b)
        for t in range(3):
            dist = jnp.abs(rel_base - t * ATT_HALF)
            bias_ref[t] = jnp.where(dist <= ATT_HALF, -slope * dist.astype(F32), NEG_INF)

    def body(it, carry):
        r = lax.shift_right_logical(it, n_blocks.bit_length() - 1)
        i = it & (n_blocks - 1)
        q0 = pl.multiple_of(i * ATT_QB, ATT_QB)
        k0 = pl.multiple_of(jnp.clip(i * ATT_QB - ATT_HALF, 0, seq_len - ATT_KW), ATT_HALF)
        q2 = q_ref[0, r, pl.ds(q0, ATT_QB), :] * jnp.asarray(ATT_HEAD_DIM ** -0.5, BF16)
        zero = jnp.zeros_like(q2)
        qs = jnp.concatenate([jnp.where(is_a, q2, zero), jnp.where(is_a, zero, q2)], axis=0)
        kk = k_ref[0, r, pl.ds(k0, ATT_KW), :]
        vv = v_ref[0, r, pl.ds(k0, ATT_KW), :]
        s = lax.dot_general(qs, kk, (((1,), (1,)), ((), ())), preferred_element_type=F32)
        s = s + bias_ref[lax.shift_right_logical(q0 - k0, ATT_HALF.bit_length() - 1)]
        m = jnp.max(s, axis=-1, keepdims=True)
        p = jnp.exp(s - m)
        l = jnp.sum(p, axis=-1, keepdims=True)
        pv = jnp.dot(p.astype(BF16), vv, preferred_element_type=F32)
        o = pv * (1.0 / l)
        lse = jnp.broadcast_to(m + jnp.log(l), (2 * ATT_QB, LANES))
        o_ref[0, r, pl.ds(q0, ATT_QB), :] = jnp.where(is_a, o[:ATT_QB], o[ATT_QB:]).astype(BF16)
        lse_ref[0, r, pl.ds(q0, ATT_QB), :] = jnp.where(is_a, lse[:ATT_QB], lse[ATT_QB:])
        return carry

    lax.fori_loop(0, dil * n_blocks, body, 0, unroll=ATT_UNROLL)


def _attention_group(qkv, slopes, *, dil):
    batch, d, seq_len, _ = qkv.shape
    n_pairs = ATT_OUT // LANES

    def spec(col0):
        return pl.BlockSpec((1, d, seq_len, LANES), lambda sp, b: (b, 0, 0, col0 + sp))

    shape = (batch, d, seq_len, ATT_OUT)
    return pl.pallas_call(
        functools.partial(_attn_kernel, seq_len=seq_len, dil=dil),
        grid=(n_pairs, batch),
        in_specs=[pl.BlockSpec(memory_space=pltpu.SMEM), spec(0), spec(n_pairs), spec(2 * n_pairs)],
        out_specs=[spec(0), spec(0)],
        out_shape=[jax.ShapeDtypeStruct(shape, BF16), jax.ShapeDtypeStruct(shape, F32)],
        scratch_shapes=[pltpu.VMEM((3, 2 * ATT_QB, ATT_KW), F32)],
        compiler_params=_cparams(("arbitrary", "arbitrary")),
        name=f"attn_d{dil}",
    )(slopes, qkv, qkv, qkv)


def _blend_groups(ol_refs, so_ref, sl_ref, tm):
    (o0_ref, l0_ref), rest = ol_refs[0], ol_refs[1:]
    outs = [o0_ref[0, 0].astype(F32)]
    lses = [l0_ref[0, 0]]
    for d, (o_ref, l_ref) in zip(DILATIONS[1:], rest):
        for r in range(d):
            for cb in range(ATT_OUT // LANES):
                cols = slice(cb * LANES, (cb + 1) * LANES)
                so_ref[cb, pl.ds(r, tm // d, stride=d), :] = o_ref[0, r, :, cols].astype(F32)
                sl_ref[cb, pl.ds(r, tm // d, stride=d), :] = l_ref[0, r, :, cols]
        outs.append(jnp.concatenate([so_ref[cb] for cb in range(ATT_OUT // LANES)], axis=1))
        lses.append(jnp.concatenate([sl_ref[cb] for cb in range(ATT_OUT // LANES)], axis=1))
    mx = jnp.maximum(jnp.maximum(lses[0], lses[1]), lses[2])
    ws = [jnp.exp(l - mx) for l in lses]
    num = ws[0] * outs[0] + ws[1] * outs[1] + ws[2] * outs[2]
    return (num / (ws[0] + ws[1] + ws[2])).astype(BF16)


HALO = 8
SSD_SUB = 2
SSD_ROWS = SSD_SUB * SSD_CHUNK


def _expand_rows(fac, col0, is_a):
    parts = []
    for j in range(SSD_HPG // 2):
        a = fac[:, col0 + 2 * j:col0 + 2 * j + 1]
        b = fac[:, col0 + 2 * j + 1:col0 + 2 * j + 2]
        parts.append(jnp.where(is_a, a, b))
    return jnp.concatenate(parts, axis=1)


def _split2(v):
    hi = v.astype(BF16)
    lo = (v - hi.astype(F32)).astype(BF16)
    return jnp.concatenate([hi, lo], axis=1)


def _split3(v):
    hi = v.astype(BF16)
    r1 = v - hi.astype(F32)
    mid = r1.astype(BF16)
    lo = (r1 - mid.astype(F32)).astype(BF16)
    return jnp.concatenate([hi, mid, lo], axis=1)


def _ssd_kernel(z_ref, xbc_ref, prev_ref, next_ref, dt_ref, cw_ref, cb_ref, dtb_ref, alog_ref,
                dskip_ref, nw_ref, out_ref, ext_ref, xc_ref, yf_ref, st_ref, sel_ref, *, n_blocks):
    q = SSD_CHUNK
    p = pl.program_id(1)
    c = pl.program_id(2)
    fwd = p == 0
    cc = jnp.where(fwd, c, n_blocks - 1 - c)
    row0 = pl.multiple_of(cc * SSD_ROWS, SSD_ROWS)

    @pl.when(jnp.logical_and(fwd, c == 0))
    def _():
        yf_ref[...] = jnp.zeros_like(yf_ref)

    @pl.when(c == 0)
    def _():
        st_ref[...] = jnp.zeros_like(st_ref)
        src = lax.broadcasted_iota(jnp.int32, (2 * LANES, SSD_INNER), 0) & (LANES - 1)
        dst = lax.shift_right_logical(lax.broadcasted_iota(jnp.int32, (2 * LANES, SSD_INNER), 1),
                                      SSD_HEAD_DIM.bit_length() - 1)
        sel_ref[...] = jnp.where(src == dst, 1.0, 0.0).astype(BF16)

    @pl.when(fwd)
    def _():
        for cb in range(CONV_CH // LANES):
            cols = slice(cb * LANES, (cb + 1) * LANES)
            ext_ref[cb, 0:HALO, :] = jnp.where(cc == 0, 0.0, prev_ref[:, cols])
            ext_ref[cb, HALO:HALO + SSD_ROWS, :] = xbc_ref[:, cols]
            ext_ref[cb, HALO + SSD_ROWS:, :] = jnp.where(cc == n_blocks - 1, 0.0, next_ref[:, cols])
            acc = jnp.broadcast_to(cb_ref[:, cols], (SSD_ROWS, LANES))
            for k in range(SSD_CONV):
                acc = acc + cw_ref[k:k + 1, cols] * ext_ref[cb, pl.ds(HALO - SSD_CONV // 2 + k, SSD_ROWS), :]
            xc_ref[pl.ds(row0, SSD_ROWS), cols] = _silu(acc)

    gn = SSD_GROUPS * SSD_STATE
    half = SSD_INNER // SSD_GROUPS
    ri = lax.broadcasted_iota(jnp.int32, (q, q), 0)
    ci = lax.broadcasted_iota(jnp.int32, (q, q), 1)
    before = (ci - ri) * jnp.where(fwd, 1, -1) <= 0
    cum_lhs = jnp.concatenate([before.astype(BF16), jnp.ones((q, q), BF16)], axis=0)
    is_a = lax.broadcasted_iota(jnp.int32, (q, LANES), 1) < SSD_HEAD_DIM
    neg_a = -jnp.exp(alog_ref[...])

    for k in range(SSD_SUB):
        off = pl.multiple_of(jnp.where(fwd, k, SSD_SUB - 1 - k) * q, q)
        rows = pl.ds(row0 + off, q)
        xs = xc_ref[rows, 0:SSD_INNER]

        dt_all = _softplus(dt_ref[pl.ds(off, q), :] + dtb_ref[...])
        dta_all = dt_all * neg_a
        dt_d = jnp.where(fwd, dt_all, pltpu.roll(dt_all, LANES - SSD_HEADS, 1))
        dta_d = jnp.where(fwd, dta_all, pltpu.roll(dta_all, LANES - SSD_HEADS, 1))
        sums = jnp.dot(cum_lhs, _split3(dta_d), preferred_element_type=F32)
        sums = sums[:, 0:LANES] + sums[:, LANES:2 * LANES] + sums[:, 2 * LANES:]
        cum = sums[:q]
        total = sums[q:]
        wide = jnp.dot(jnp.concatenate([_split2(jnp.exp(cum)), _split2(jnp.exp(total - cum) * dt_d)], axis=0),
                       sel_ref[...], preferred_element_type=F32)
        e_cum_w = wide[:q]
        d_state_w = wide[q:]
        d_chunk = jnp.exp(total[0:8])
        log_dt_t = jnp.log(dt_d).T
        shifted_t = cum.T - log_dt_t

        ys = []
        for g in range(SSD_GROUPS):
            col0 = g * SSD_HPG
            gcols = slice(g * half, (g + 1) * half)
            xg = xs[:, gcols]
            bg = xc_ref[rows, SSD_INNER + g * SSD_STATE:SSD_INNER + (g + 1) * SSD_STATE]
            cg = xc_ref[rows, SSD_INNER + gn + g * SSD_STATE:SSD_INNER + gn + (g + 1) * SSD_STATE]
            xg16 = xg.astype(BF16)
            bg16 = bg.astype(BF16)
            cg16 = cg.astype(BF16)
            cbm = lax.dot_general(cg16, bg16, (((1,), (1,)), ((), ())), preferred_element_type=F32)
            cbm = jnp.where(before, cbm, 0.0)
            y_parts = []
            for j in range(SSD_HPG // 2):
                ms = []
                for h in (2 * j, 2 * j + 1):
                    col = col0 + h
                    diff = cum[:, col:col + 1] - shifted_t[col:col + 1, :]
                    ms.append((cbm * jnp.exp(jnp.minimum(diff, log_dt_t[col:col + 1, :]))).astype(BF16))
                xp = xg16[:, j * LANES:(j + 1) * LANES]
                zero = jnp.zeros_like(xp)
                xdiag = jnp.concatenate([jnp.where(is_a, xp, zero), jnp.where(is_a, zero, xp)], axis=0)
                y_parts.append(jnp.dot(jnp.concatenate(ms, axis=1), xdiag, preferred_element_type=F32))
            y_diag = jnp.concatenate(y_parts, axis=1)
            st = st_ref[g]
            y_off = jnp.dot(cg16, st.astype(BF16), preferred_element_type=F32) * e_cum_w[:, gcols]
            xd = (xg * d_state_w[:, gcols]).astype(BF16)
            new = jnp.dot(bg.T.astype(BF16), xd, preferred_element_type=F32)
            keep = _expand_rows(d_chunk, col0, is_a[0:8])
            st_ref[g] = (st.reshape(SSD_STATE // 8, 8, -1) * keep[None]).reshape(st.shape) + new
            ys.append(y_diag + y_off)
        y = jnp.concatenate(ys, axis=1)
        yf_ref[rows, :] += y

    @pl.when(jnp.logical_not(fwd))
    def _():
        rows = pl.ds(row0, SSD_ROWS)
        yt = (yf_ref[rows, :] + xc_ref[rows, 0:SSD_INNER] * dskip_ref[...]) * _silu(z_ref[...])
        for g in range(SSD_GROUPS):
            v = yt[:, g * half:(g + 1) * half]
            ms = jnp.mean(v * v, axis=-1, keepdims=True)
            out_ref[:, g * half:(g + 1) * half] = (
                v * lax.rsqrt(ms + RMS_EPS) * nw_ref[:, g * half:(g + 1) * half]).astype(BF16)


def _ssd(zux, dt, cw, cb, dtb, alog, dskip, nw, *, batch, seq):
    t = batch * seq
    rows = SSD_ROWS
    n_blocks = seq // rows
    hb = rows // HALO

    def row_blk(b, p, c):
        return b * n_blocks + jnp.where(p == 0, c, n_blocks - 1 - c)

    def conv_blk(b, p, c):
        return b * n_blocks + jnp.where(p == 0, c, n_blocks - 1)

    def prev_blk(b, p, c):
        return jnp.maximum(conv_blk(b, p, c) * hb - 1, b * n_blocks * hb)

    def next_blk(b, p, c):
        return jnp.minimum((conv_blk(b, p, c) + 1) * hb, (b + 1) * n_blocks * hb - 1)

    def z_blk(b, p, c):
        return jnp.where(p == 0, b * n_blocks + n_blocks - 1, row_blk(b, p, c))

    def const(shape):
        return pl.BlockSpec(shape, lambda b, p, c: (0, 0))

    return pl.pallas_call(
        functools.partial(_ssd_kernel, n_blocks=n_blocks),
        grid=(batch, 2, n_blocks),
        in_specs=[
            pl.BlockSpec((rows, SSD_INNER), lambda b, p, c: (z_blk(b, p, c), 0)),
            pl.BlockSpec((rows, CONV_CH), lambda b, p, c: (conv_blk(b, p, c), 1)),
            pl.BlockSpec((HALO, CONV_CH), lambda b, p, c: (prev_blk(b, p, c), 1)),
            pl.BlockSpec((HALO, CONV_CH), lambda b, p, c: (next_blk(b, p, c), 1)),
            pl.BlockSpec((rows, LANES), lambda b, p, c: (row_blk(b, p, c), 0)),
            const((8, CONV_CH)), const((1, CONV_CH)), const((1, LANES)), const((1, LANES)),
            const((1, SSD_INNER)), const((1, SSD_INNER)),
        ],
        out_specs=pl.BlockSpec((rows, SSD_INNER), lambda b, p, c: (z_blk(b, p, c), 0)),
        out_shape=jax.ShapeDtypeStruct((t, SSD_INNER), BF16),
        scratch_shapes=[
            pltpu.VMEM((CONV_CH // LANES, rows + 2 * HALO, LANES), F32),
            pltpu.VMEM((seq, CONV_CH), F32),
            pltpu.VMEM((seq, SSD_INNER), F32),
            pltpu.VMEM((SSD_GROUPS, SSD_STATE, SSD_INNER // SSD_GROUPS), F32),
            pltpu.VMEM((2 * LANES, SSD_INNER), BF16),
        ],
        compiler_params=_cparams(("arbitrary", "arbitrary", "arbitrary")),
        name="ssd",
    )(zux, zux, zux, zux, dt, cw, cb, dtb, alog, dskip, nw)


POOL_PAD = 16
POOL_ROWS = 512


def _pool_kernel(u_ref, w_ref, sc_ref, out_ref, pad_ref, *, seq):
    zeros = jnp.zeros((POOL_PAD, POOL_GROUP), F32)
    for g in range(len(POOL_WINDOWS)):
        pad_ref[g, 0:POOL_PAD, :] = zeros
        pad_ref[g, POOL_PAD + seq:, :] = zeros
        pad_ref[g, POOL_PAD:POOL_PAD + seq, :] = u_ref[:, g * POOL_GROUP:(g + 1) * POOL_GROUP]
    for r0 in range(0, seq, POOL_ROWS):
        pos = r0 + lax.broadcasted_iota(jnp.int32, (POOL_ROWS, POOL_GROUP), 0)
        for g, w in enumerate(POOL_WINDOWS):
            cols = slice(g * POOL_GROUP, (g + 1) * POOL_GROUP)
            u = pad_ref[g, POOL_PAD + r0:POOL_PAD + r0 + POOL_ROWS, :]
            s = None
            for j in range(-(w // 2), w // 2):
                term = pad_ref[g, POOL_PAD + r0 + j:POOL_PAD + r0 + j + POOL_ROWS, :]
                s = term if s is None else s + term
            cnt = jnp.minimum(pos + w // 2, seq) - jnp.maximum(pos - w // 2, 0)
            mean = s / cnt.astype(F32)
            y = jnp.dot((mean - u).astype(BF16), w_ref[g], preferred_element_type=F32)
            out_ref[r0:r0 + POOL_ROWS, cols] = (y * sc_ref[:, cols]).astype(BF16)


def _pool(zux, pw, psc, *, batch, seq):
    t = batch * seq
    return pl.pallas_call(
        functools.partial(_pool_kernel, seq=seq),
        scratch_shapes=[pltpu.VMEM((len(POOL_WINDOWS), seq + 2 * POOL_PAD, POOL_GROUP), F32)],
        grid=(batch,),
        in_specs=[
            pl.BlockSpec((seq, POOL_WIDTH), lambda b: (b, SSD_INNER // POOL_WIDTH)),
            pl.BlockSpec((len(POOL_WINDOWS), POOL_GROUP, POOL_GROUP), lambda b: (0, 0, 0)),
            pl.BlockSpec((1, POOL_WIDTH), lambda b: (0, 0)),
        ],
        out_specs=pl.BlockSpec((seq, POOL_WIDTH), lambda b: (b, 0)),
        out_shape=jax.ShapeDtypeStruct((t, POOL_WIDTH), BF16),
        compiler_params=_cparams(("arbitrary",)),
        name="pool",
    )(zux, pw, psc)


def _outproj_kernel(x_ref, o0_ref, l0_ref, o1_ref, l1_ref, o2_ref, l2_ref, s_ref, p_ref, w_ref, o_ref,
                    so_ref, sl_ref, *, tm):
    s0, p0 = ATT_OUT, ATT_OUT + SSD_INNER
    att = _blend_groups(((o0_ref, l0_ref), (o1_ref, l1_ref), (o2_ref, l2_ref)), so_ref, sl_ref, tm)
    acc = jnp.dot(att, w_ref[0, 0:s0, :], preferred_element_type=F32)
    acc = acc + jnp.dot(s_ref[...], w_ref[0, s0:p0, :], preferred_element_type=F32)
    acc = acc + jnp.dot(p_ref[...], w_ref[0, p0:, :], preferred_element_type=F32)
    o_ref[...] = x_ref[...] + acc


def _outproj(x2, ol, ssd, pool, wo, layer, *, batch, seq, tm):
    t = x2.shape[0]
    per_b = seq // tm
    in_specs = [pl.BlockSpec((tm, D_MODEL), lambda i: (i, 0))]
    args = [x2]
    for d, (o, l) in zip(DILATIONS, ol):
        spec = pl.BlockSpec((1, d, tm // d, ATT_OUT), lambda i: (i // per_b, 0, i % per_b, 0))
        in_specs += [spec, spec]
        args += [o, l]
    in_specs += [
        pl.BlockSpec((tm, SSD_INNER), lambda i: (i, 0)),
        pl.BlockSpec((tm, POOL_WIDTH), lambda i: (i, 0)),
        pl.BlockSpec((1, MIX_WIDTH, D_MODEL), lambda i: (layer, 0, 0)),
    ]
    return pl.pallas_call(
        functools.partial(_outproj_kernel, tm=tm),
        grid=(t // tm,),
        in_specs=in_specs,
        out_specs=pl.BlockSpec((tm, D_MODEL), lambda i: (i, 0)),
        out_shape=jax.ShapeDtypeStruct((t, D_MODEL), F32),
        scratch_shapes=[pltpu.VMEM((ATT_OUT // LANES, tm, LANES), F32)] * 2,
        compiler_params=_cparams(("arbitrary",)),
        name="outproj",
    )(*args, ssd, pool, wo)


def _ffn_kernel(x_ref, nw_ref, wg_ref, wu_ref, wd_ref, fw_ref, o_ref, h_ref, *, n_ff, final):
    j = pl.program_id(1)

    @pl.when(j == 0)
    def _():
        x = x_ref[...]
        ms = jnp.mean(x * x, axis=-1, keepdims=True)
        h_ref[...] = (x * lax.rsqrt(ms + RMS_EPS) * nw_ref[...]).astype(BF16)
        o_ref[...] = x

    h = h_ref[...]
    gate = jnp.dot(h, wg_ref[0], preferred_element_type=F32)
    up = jnp.dot(h, wu_ref[0], preferred_element_type=F32)
    act = (_silu(gate) * up).astype(BF16)
    o_ref[...] += jnp.dot(act, wd_ref[0], preferred_element_type=F32)

    if final:
        @pl.when(j == n_ff - 1)
        def _():
            y = o_ref[...]
            ms = jnp.mean(y * y, axis=-1, keepdims=True)
            o_ref[...] = y * lax.rsqrt(ms + RMS_EPS) * fw_ref[...]


def _ffn(x2, nw, wg, wu, wd, fw, layer, *, tm, tf, final):
    t = x2.shape[0]
    n_ff = D_FF // tf
    return pl.pallas_call(
        functools.partial(_ffn_kernel, n_ff=n_ff, final=final),
        grid=(t // tm, n_ff),
        in_specs=[
            pl.BlockSpec((tm, D_MODEL), lambda i, j: (i, 0)),
            pl.BlockSpec((1, D_MODEL), lambda i, j: (0, 0)),
            pl.BlockSpec((1, D_MODEL, tf), lambda i, j: (layer, 0, j)),
            pl.BlockSpec((1, D_MODEL, tf), lambda i, j: (layer, 0, j)),
            pl.BlockSpec((1, tf, D_MODEL), lambda i, j: (layer, j, 0)),
            pl.BlockSpec((1, D_MODEL), lambda i, j: (0, 0)),
        ],
        out_specs=pl.BlockSpec((tm, D_MODEL), lambda i, j: (i, 0)),
        out_shape=jax.ShapeDtypeStruct((t, D_MODEL), F32),
        scratch_shapes=[pltpu.VMEM((tm, D_MODEL), BF16)],
        compiler_params=_cparams(("arbitrary", "arbitrary")),
        name="ffn_final" if final else "ffn",
    )(x2, nw, wg, wu, wd, fw)


PACK_W = 512
PACK_U_TILE = 11


def _pack_kernel(wt_ref, tail_ref, o_ref, wdt_ref):
    j = pl.program_id(0)
    n_dt = 2 * SSD_HEADS

    @pl.when(j == 0)
    def _():
        lane = lax.broadcasted_iota(jnp.int32, (D_MODEL, LANES), 1)
        wdt_ref[...] = jnp.where(lane < n_dt, tail_ref[:, 0:LANES], 0.0).astype(BF16)

    @pl.when(j != PACK_U_TILE)
    def _():
        o_ref[...] = wt_ref[0].T.astype(BF16)

    @pl.when(j == PACK_U_TILE)
    def _():
        o_ref[...] = tail_ref[:, n_dt:n_dt + POOL_WIDTH].astype(BF16)


def _pack_in_weights(w_in, w_in_t, layer):
    z0 = 3 * ATT_WIDTH
    dt0 = z0 + SSD_INNER + CONV_CH
    n_tail = w_in.shape[2] - dt0
    n_qkv = z0 // PACK_W

    def src_tile(j):
        qkv = (j % N_DIL) * 3 + j // N_DIL
        return jnp.where(j < n_qkv, qkv, jnp.where(j < PACK_U_TILE, j, j - 1))

    return pl.pallas_call(
        _pack_kernel,
        grid=(N_IN_TILES * IN_TILE // PACK_W,),
        in_specs=[
            pl.BlockSpec((1, PACK_W, D_MODEL), lambda j: (layer, src_tile(j), 0)),
            pl.BlockSpec((D_MODEL, n_tail), lambda j: (0, 0)),
        ],
        out_specs=[pl.BlockSpec((D_MODEL, PACK_W), lambda j: (0, j)),
                   pl.BlockSpec((D_MODEL, LANES), lambda j: (0, 0))],
        out_shape=[jax.ShapeDtypeStruct((D_MODEL, N_IN_TILES * IN_TILE), BF16),
                   jax.ShapeDtypeStruct((D_MODEL, LANES), BF16)],
        compiler_params=_cparams(("arbitrary",)),
        name="pack_w_in",
    )(w_in_t, w_in[layer, :, dt0:])


def _pad_lanes(v):
    return jnp.pad(v.reshape(1, -1).astype(F32), ((0, 0), (0, LANES - v.size)))


def _layer(x2, slopes, p, fw, *, batch, seq, final):
    wcat, wdt = _pack_in_weights(p["w_in"], p["w_in_t"], p["layer"])
    q0, q1, q2, zux, dt = _inproj(x2, p["norm1_w"].reshape(1, -1), wcat, wdt,
                                  batch=batch, seq=seq, tm=IN_TM)
    ol = [_attention_group(qkv, slopes[g], dil=d) for g, (d, qkv) in enumerate(zip(DILATIONS, (q0, q1, q2)))]
    cw = jnp.pad(p["conv_w"].astype(F32), ((0, 8 - SSD_CONV), (0, 0)))
    ssd = _ssd(zux, dt, cw, p["conv_b"].reshape(1, -1).astype(F32), _pad_lanes(p["dt_bias"]),
               _pad_lanes(p["a_log"]), jnp.repeat(p["d_skip"].astype(F32), SSD_HEAD_DIM).reshape(1, -1),
               p["ssd_norm_w"].reshape(1, -1).astype(F32), batch=batch, seq=seq)
    pool = _pool(zux, p["pool_w"].astype(BF16), p["pool_scale"].reshape(1, -1).astype(F32),
                 batch=batch, seq=seq)
    x2 = _outproj(x2, ol, ssd, pool, p["w_out"], p["layer"], batch=batch, seq=seq, tm=OUT_TM)
    return _ffn(x2, p["norm2_w"].reshape(1, -1), p["w_gate"], p["w_up"], p["w_down"], fw.reshape(1, -1),
                p["layer"], tm=FFN_TM, tf=FFN_TF, final=final)


def kernel(x, norm1_w, w_in, conv_w, conv_b, dt_bias, a_log, d_skip, ssd_norm_w, pool_w, pool_scale,
           w_out, norm2_w, w_gate, w_up, w_down, final_norm_w):
    batch, seq, _ = x.shape
    depth = w_in.shape[0]
    k = jnp.arange(1, ATT_HEADS + 1, dtype=F32)
    slopes = (2.0 ** (-8.0 * k / ATT_HEADS)).reshape(N_DIL, ATT_SLOTS)
    x2 = x.reshape(batch * seq, D_MODEL)
    wo, wg, wu, wd = (w.astype(BF16) for w in (w_out, w_gate, w_up, w_down))
    w_in_t = jnp.swapaxes(w_in, 1, 2)
    for l in range(depth):
        p = dict(norm1_w=norm1_w[l], w_in=w_in, w_in_t=w_in_t, layer=l, conv_w=conv_w[l], conv_b=conv_b[l], dt_bias=dt_bias[l],
                 a_log=a_log[l], d_skip=d_skip[l], ssd_norm_w=ssd_norm_w[l], pool_w=pool_w[l],
                 pool_scale=pool_scale[l], w_out=wo, norm2_w=norm2_w[l], w_gate=wg, w_up=wu, w_down=wd)
        x2 = _layer(x2, slopes, p, final_norm_w, batch=batch, seq=seq, final=(l == depth - 1))
    return x2.reshape(batch, seq, D_MODEL)
```

```python
import functools
import math

import jax
import jax.numpy as jnp
from jax import lax
from jax.experimental import pallas as pl
from jax.experimental.pallas import tpu as pltpu

F32 = jnp.float32
BF16 = jnp.bfloat16

D_MODEL = 2048
ATT_HEAD_DIM = 64
ATT_SLOTS = 8
DILATIONS = (1, 4, 16)
ATT_HALF = 64
N_DIL = len(DILATIONS)
ATT_HEADS = ATT_SLOTS * N_DIL
ATT_WIDTH = ATT_HEADS * ATT_HEAD_DIM
ATT_OUT = ATT_SLOTS * ATT_HEAD_DIM
SSD_HEAD_DIM = 64
SSD_INNER = D_MODEL // 2
SSD_HEADS = SSD_INNER // SSD_HEAD_DIM
SSD_GROUPS = 2
SSD_HPG = SSD_HEADS // SSD_GROUPS
SSD_STATE = 128
SSD_CONV = 5
SSD_CHUNK = 128
CONV_CH = SSD_INNER + 2 * SSD_GROUPS * SSD_STATE
POOL_WINDOWS = (2, 4, 8, 16)
POOL_GROUP = 128
POOL_WIDTH = POOL_GROUP * len(POOL_WINDOWS)
MIX_WIDTH = ATT_OUT + SSD_INNER + POOL_WIDTH
D_FF = 5632
RMS_EPS = 1e-6
NEG_INF = -1e30

LANES = 128
IN_TILE = 768
IN_TILES_PER_GROUP = 2
N_IN_TILES = 5 * IN_TILES_PER_GROUP
IN_TM = 1024
VMEM_LIMIT = 56 * 1024 * 1024
OUT_TM = 512
FFN_TM = 1024
FFN_TF = 512


def _cparams(sem):
    return pltpu.CompilerParams(dimension_semantics=sem, vmem_limit_bytes=VMEM_LIMIT)


def _silu(x):
    half = 0.5 * x
    return half + half * jnp.tanh(half)


def _softplus(x):
    return jnp.maximum(x, 0.0) + jnp.log(1.0 + jnp.exp(-jnp.abs(x)))


def _inproj_kernel(x_ref, nw_ref, w_ref, wdt_ref, q0_ref, q1_ref, q2_ref, zux_ref, dt_ref,
                   h_ref, stage_ref, *, tm):
    j = pl.program_id(1)
    jg = j // IN_TILES_PER_GROUP

    @pl.when(j == 0)
    def _():
        x = x_ref[...]
        ms = jnp.mean(x * x, axis=-1, keepdims=True)
        h = x * lax.rsqrt(ms + RMS_EPS) * nw_ref[...]
        h_ref[...] = h.astype(BF16)
        dt_ref[...] = jnp.dot(h_ref[...], wdt_ref[...], preferred_element_type=F32)

    def proj():
        return jnp.dot(h_ref[...], w_ref[...], preferred_element_type=F32)

    @pl.when(jg == 0)
    def _():
        q0_ref[0, 0] = proj().astype(BF16)

    for g, (d, out_ref) in enumerate(zip(DILATIONS, (q0_ref, q1_ref, q2_ref))):
        if g == 0:
            continue

        @pl.when(jg == g)
        def _(d=d, out_ref=out_ref):
            res = proj()
            for cb in range(IN_TILE // LANES):
                stage_ref[cb] = res[:, cb * LANES:(cb + 1) * LANES]
            for r in range(d):
                for cb in range(IN_TILE // LANES):
                    out_ref[0, r, :, cb * LANES:(cb + 1) * LANES] = (
                        stage_ref[cb, pl.ds(r, tm // d, stride=d), :].astype(BF16))

    @pl.when(jg >= N_DIL)
    def _():
        zux_ref[...] = proj()


def _inproj(x2, nw, wcat, wdt, *, batch, seq, tm):
    t = x2.shape[0]
    per_b = seq // tm
    grid = (t // tm, N_IN_TILES)
    tpg = IN_TILES_PER_GROUP

    def qspec(g, d):
        return pl.BlockSpec((1, d, tm // d, IN_TILE),
                            lambda i, j: (i // per_b, 0, i % per_b, jnp.clip(j - g * tpg, 0, tpg - 1)))

    out_shape = [jax.ShapeDtypeStruct((batch, d, seq // d, tpg * IN_TILE), BF16) for d in DILATIONS]
    out_shape += [jax.ShapeDtypeStruct((t, 2 * tpg * IN_TILE), F32), jax.ShapeDtypeStruct((t, LANES), F32)]
    return pl.pallas_call(
        functools.partial(_inproj_kernel, tm=tm),
        grid=grid,
        in_specs=[
            pl.BlockSpec((tm, D_MODEL), lambda i, j: (i, 0)),
            pl.BlockSpec((1, D_MODEL), lambda i, j: (0, 0)),
            pl.BlockSpec((D_MODEL, IN_TILE), lambda i, j: (0, j)),
            pl.BlockSpec((D_MODEL, LANES), lambda i, j: (0, 0)),
        ],
        out_specs=[qspec(g, d) for g, d in enumerate(DILATIONS)] + [
            pl.BlockSpec((tm, IN_TILE), lambda i, j: (i, jnp.maximum(j - N_DIL * tpg, 0))),
            pl.BlockSpec((tm, LANES), lambda i, j: (i, 0)),
        ],
        out_shape=out_shape,
        scratch_shapes=[pltpu.VMEM((tm, D_MODEL), BF16), pltpu.VMEM((IN_TILE // LANES, tm, LANES), F32)],
        compiler_params=_cparams(("arbitrary", "arbitrary")),
        name="inproj",
    )(x2, nw, wcat, wdt)


ATT_QB = 128
ATT_KW = 256
ATT_UNROLL = 32


def _attn_kernel(slopes_ref, q_ref, k_ref, v_ref, o_ref, lse_ref, bias_ref, *, seq_len, dil):
    sp = pl.program_id(0)
    is_a = lax.broadcasted_iota(jnp.int32, (ATT_QB, LANES), 1) < ATT_HEAD_DIM
    n_blocks = seq_len // ATT_QB

    @pl.when(pl.program_id(1) == 0)
    def _():
        slope_a = slopes_ref[2 * sp] * float(dil)
        slope_b = slopes_ref[2 * sp + 1] * float(dil)
        rows = lax.broadcasted_iota(jnp.int32, (2 * ATT_QB, ATT_KW), 0)
        cols = lax.broadcasted_iota(jnp.int32, (2 * ATT_QB, ATT_KW), 1)
        rel_base = cols - (rows & (ATT_QB - 1))
        slope = jnp.where(rows < ATT_QB, slope_a, slope_b)
        for t in range(3):
            dist = jnp.abs(rel_base - t * ATT_HALF)
            bias_ref[t] = jnp.where(dist <= ATT_HALF, -slope * dist.astype(F32), NEG_INF)

    def body(it, carry):
        r = lax.shift_right_logical(it, n_blocks.bit_length() - 1)
        i = it & (n_blocks - 1)
        q0 = pl.multiple_of(i * ATT_QB, ATT_QB)
        k0 = pl.multiple_of(jnp.clip(i * ATT_QB - ATT_HALF, 0, seq_len - ATT_KW), ATT_HALF)
        q2 = q_ref[0, r, pl.ds(q0, ATT_QB), :] * jnp.asarray(ATT_HEAD_DIM ** -0.5, BF16)
        zero = jnp.zeros_like(q2)
        qs = jnp.concatenate([jnp.where(is_a, q2, zero), jnp.where(is_a, zero, q2)], axis=0)
        kk = k_ref[0, r, pl.ds(k0, ATT_KW), :]
        vv = v_ref[0, r, pl.ds(k0, ATT_KW), :]
        s = lax.dot_general(qs, kk, (((1,), (1,)), ((), ())), preferred_element_type=F32)
        s = s + bias_ref[lax.shift_right_logical(q0 - k0, ATT_HALF.bit_length() - 1)]
        m = jnp.max(s, axis=-1, keepdims=True)
        p = jnp.exp(s - m)
        l = jnp.sum(p, axis=-1, keepdims=True)
        pv = jnp.dot(p.astype(BF16), vv, preferred_element_type=F32)
        o = pv * (1.0 / l)
        lse = jnp.broadcast_to(m + jnp.log(l), (2 * ATT_QB, LANES))
        o_ref[0, r, pl.ds(q0, ATT_QB), :] = jnp.where(is_a, o[:ATT_QB], o[ATT_QB:]).astype(BF16)
        lse_ref[0, r, pl.ds(q0, ATT_QB), :] = jnp.where(is_a, lse[:ATT_QB], lse[ATT_QB:])
        return carry

    lax.fori_loop(0, dil * n_blocks, body, 0, unroll=ATT_UNROLL)


def _attention_group(qkv, slopes, *, dil):
    batch, d, seq_len, _ = qkv.shape
    n_pairs = ATT_OUT // LANES

    def spec(col0):
        return pl.BlockSpec((1, d, seq_len, LANES), lambda sp, b: (b, 0, 0, col0 + sp))

    shape = (batch, d, seq_len, ATT_OUT)
    return pl.pallas_call(
        functools.partial(_attn_kernel, seq_len=seq_len, dil=dil),
        grid=(n_pairs, batch),
        in_specs=[pl.BlockSpec(memory_space=pltpu.SMEM), spec(0), spec(n_pairs), spec(2 * n_pairs)],
        out_specs=[spec(0), spec(0)],
        out_shape=[jax.ShapeDtypeStruct(shape, BF16), jax.ShapeDtypeStruct(shape, F32)],
        scratch_shapes=[pltpu.VMEM((3, 2 * ATT_QB, ATT_KW), F32)],
        compiler_params=_cparams(("arbitrary", "arbitrary")),
        name=f"attn_d{dil}",
    )(slopes, qkv, qkv, qkv)


def _blend_groups(ol_refs, so_ref, sl_ref, tm):
    (o0_ref, l0_ref), rest = ol_refs[0], ol_refs[1:]
    outs = [o0_ref[0, 0].astype(F32)]
    lses = [l0_ref[0, 0]]
    for d, (o_ref, l_ref) in zip(DILATIONS[1:], rest):
        for r in range(d):
            for cb in range(ATT_OUT // LANES):
                cols = slice(cb * LANES, (cb + 1) * LANES)
                so_ref[cb, pl.ds(r, tm // d, stride=d), :] = o_ref[0, r, :, cols].astype(F32)
                sl_ref[cb, pl.ds(r, tm // d, stride=d), :] = l_ref[0, r, :, cols]
        outs.append(jnp.concatenate([so_ref[cb] for cb in range(ATT_OUT // LANES)], axis=1))
        lses.append(jnp.concatenate([sl_ref[cb] for cb in range(ATT_OUT // LANES)], axis=1))
    mx = jnp.maximum(jnp.maximum(lses[0], lses[1]), lses[2])
    ws = [jnp.exp(l - mx) for l in lses]
    num = ws[0] * outs[0] + ws[1] * outs[1] + ws[2] * outs[2]
    return (num / (ws[0] + ws[1] + ws[2])).astype(BF16)


HALO = 8
SSD_SUB = 2
SSD_ROWS = SSD_SUB * SSD_CHUNK


def _expand_rows(fac, col0, is_a):
    parts = []
    for j in range(SSD_HPG // 2):
        a = fac[:, col0 + 2 * j:col0 + 2 * j + 1]
        b = fac[:, col0 + 2 * j + 1:col0 + 2 * j + 2]
        parts.append(jnp.where(is_a, a, b))
    return jnp.concatenate(parts, axis=1)


def _split2(v):
    hi = v.astype(BF16)
    lo = (v - hi.astype(F32)).astype(BF16)
    return jnp.concatenate([hi, lo], axis=1)


def _split3(v):
    hi = v.astype(BF16)
    r1 = v - hi.astype(F32)
    mid = r1.astype(BF16)
    lo = (r1 - mid.astype(F32)).astype(BF16)
    return jnp.concatenate([hi, mid, lo], axis=1)


def _ssd_kernel(z_ref, xbc_ref, prev_ref, next_ref, dt_ref, cw_ref, cb_ref, dtb_ref, alog_ref,
                dskip_ref, nw_ref, out_ref, ext_ref, xc_ref, yf_ref, st_ref, sel_ref, *, n_blocks):
    q = SSD_CHUNK
    p = pl.program_id(1)
    c = pl.program_id(2)
    fwd = p == 0
    cc = jnp.where(fwd, c, n_blocks - 1 - c)
    row0 = pl.multiple_of(cc * SSD_ROWS, SSD_ROWS)

    @pl.when(jnp.logical_and(fwd, c == 0))
    def _():
        yf_ref[...] = jnp.zeros_like(yf_ref)

    @pl.when(c == 0)
    def _():
        st_ref[...] = jnp.zeros_like(st_ref)
        src = lax.broadcasted_iota(jnp.int32, (2 * LANES, SSD_INNER), 0) & (LANES - 1)
        dst = lax.shift_right_logical(lax.broadcasted_iota(jnp.int32, (2 * LANES, SSD_INNER), 1),
                                      SSD_HEAD_DIM.bit_length() - 1)
        sel_ref[...] = jnp.where(src == dst, 1.0, 0.0).astype(BF16)

    @pl.when(fwd)
    def _():
        for cb in range(CONV_CH // LANES):
            cols = slice(cb * LANES, (cb + 1) * LANES)
            ext_ref[cb, 0:HALO, :] = jnp.where(cc == 0, 0.0, prev_ref[:, cols])
            ext_ref[cb, HALO:HALO + SSD_ROWS, :] = xbc_ref[:, cols]
            ext_ref[cb, HALO + SSD_ROWS:, :] = jnp.where(cc == n_blocks - 1, 0.0, next_ref[:, cols])
            acc = jnp.broadcast_to(cb_ref[:, cols], (SSD_ROWS, LANES))
            for k in range(SSD_CONV):
                acc = acc + cw_ref[k:k + 1, cols] * ext_ref[cb, pl.ds(HALO - SSD_CONV // 2 + k, SSD_ROWS), :]
            xc_ref[pl.ds(row0, SSD_ROWS), cols] = _silu(acc)

    gn = SSD_GROUPS * SSD_STATE
    half = SSD_INNER // SSD_GROUPS
    ri = lax.broadcasted_iota(jnp.int32, (q, q), 0)
    ci = lax.broadcasted_iota(jnp.int32, (q, q), 1)
    before = (ci - ri) * jnp.where(fwd, 1, -1) <= 0
    cum_lhs = jnp.concatenate([before.astype(BF16), jnp.ones((q, q), BF16)], axis=0)
    is_a = lax.broadcasted_iota(jnp.int32, (q, LANES), 1) < SSD_HEAD_DIM
    neg_a = -jnp.exp(alog_ref[...])

    for k in range(SSD_SUB):
        off = pl.multiple_of(jnp.where(fwd, k, SSD_SUB - 1 - k) * q, q)
        rows = pl.ds(row0 + off, q)
        xs = xc_ref[rows, 0:SSD_INNER]

        dt_all = _softplus(dt_ref[pl.ds(off, q), :] + dtb_ref[...])
        dta_all = dt_all * neg_a
        dt_d = jnp.where(fwd, dt_all, pltpu.roll(dt_all, LANES - SSD_HEADS, 1))
        dta_d = jnp.where(fwd, dta_all, pltpu.roll(dta_all, LANES - SSD_HEADS, 1))
        sums = jnp.dot(cum_lhs, _split3(dta_d), preferred_element_type=F32)
        sums = sums[:, 0:LANES] + sums[:, LANES:2 * LANES] + sums[:, 2 * LANES:]
        cum = sums[:q]
        total = sums[q:]
        wide = jnp.dot(jnp.concatenate([_split2(jnp.exp(cum)), _split2(jnp.exp(total - cum) * dt_d)], axis=0),
                       sel_ref[...], preferred_element_type=F32)
        e_cum_w = wide[:q]
        d_state_w = wide[q:]
        d_chunk = jnp.exp(total[0:8])
        log_dt_t = jnp.log(dt_d).T
        shifted_t = cum.T - log_dt_t

        ys = []
        for g in range(SSD_GROUPS):
            col0 = g * SSD_HPG
            gcols = slice(g * half, (g + 1) * half)
            xg = xs[:, gcols]
            bg = xc_ref[rows, SSD_INNER + g * SSD_STATE:SSD_INNER + (g + 1) * SSD_STATE]
            cg = xc_ref[rows, SSD_INNER + gn + g * SSD_STATE:SSD_INNER + gn + (g + 1) * SSD_STATE]
            xg16 = xg.astype(BF16)
            bg16 = bg.astype(BF16)
            cg16 = cg.astype(BF16)
            cbm = lax.dot_general(cg16, bg16, (((1,), (1,)), ((), ())), preferred_element_type=F32)
            cbm = jnp.where(before, cbm, 0.0)
            y_parts = []
            for j in range(SSD_HPG // 2):
                ms = []
                for h in (2 * j, 2 * j + 1):
                    col = col0 + h
                    diff = cum[:, col:col + 1] - shifted_t[col:col + 1, :]
                    ms.append((cbm * jnp.exp(jnp.minimum(diff, log_dt_t[col:col + 1, :]))).astype(BF16))
                xp = xg16[:, j * LANES:(j + 1) * LANES]
                zero = jnp.zeros_like(xp)
                xdiag = jnp.concatenate([jnp.where(is_a, xp, zero), jnp.where(is_a, zero, xp)], axis=0)
                y_parts.append(jnp.dot(jnp.concatenate(ms, axis=1), xdiag, preferred_element_type=F32))
            y_diag = jnp.concatenate(y_parts, axis=1)
            st = st_ref[g]
            y_off = jnp.dot(cg16, st.astype(BF16), preferred_element_type=F32) * e_cum_w[:, gcols]
            xd = (xg * d_state_w[:, gcols]).astype(BF16)
            new = jnp.dot(bg.T.astype(BF16), xd, preferred_element_type=F32)
            keep = _expand_rows(d_chunk, col0, is_a[0:8])
            st_ref[g] = (st.reshape(SSD_STATE // 8, 8, -1) * keep[None]).reshape(st.shape) + new
            ys.append(y_diag + y_off)
        y = jnp.concatenate(ys, axis=1)
        yf_ref[rows, :] += y

    @pl.when(jnp.logical_not(fwd))
    def _():
        rows = pl.ds(row0, SSD_ROWS)
        yt = (yf_ref[rows, :] + xc_ref[rows, 0:SSD_INNER] * dskip_ref[...]) * _silu(z_ref[...])
        for g in range(SSD_GROUPS):
            v = yt[:, g * half:(g + 1) * half]
            ms = jnp.mean(v * v, axis=-1, keepdims=True)
            out_ref[:, g * half:(g + 1) * half] = (
                v * lax.rsqrt(ms + RMS_EPS) * nw_ref[:, g * half:(g + 1) * half]).astype(BF16)


def _ssd(zux, dt, cw, cb, dtb, alog, dskip, nw, *, batch, seq):
    t = batch * seq
    rows = SSD_ROWS
    n_blocks = seq // rows
    hb = rows // HALO

    def row_blk(b, p, c):
        return b * n_blocks + jnp.where(p == 0, c, n_blocks - 1 - c)

    def conv_blk(b, p, c):
        return b * n_blocks + jnp.where(p == 0, c, n_blocks - 1)

    def prev_blk(b, p, c):
        return jnp.maximum(conv_blk(b, p, c) * hb - 1, b * n_blocks * hb)

    def next_blk(b, p, c):
        return jnp.minimum((conv_blk(b, p, c) + 1) * hb, (b + 1) * n_blocks * hb - 1)

    def z_blk(b, p, c):
        return jnp.where(p == 0, b * n_blocks + n_blocks - 1, row_blk(b, p, c))

    def const(shape):
        return pl.BlockSpec(shape, lambda b, p, c: (0, 0))

    return pl.pallas_call(
        functools.partial(_ssd_kernel, n_blocks=n_blocks),
        grid=(batch, 2, n_blocks),
        in_specs=[
            pl.BlockSpec((rows, SSD_INNER), lambda b, p, c: (z_blk(b, p, c), 0)),
            pl.BlockSpec((rows, CONV_CH), lambda b, p, c: (conv_blk(b, p, c), 1)),
            pl.BlockSpec((HALO, CONV_CH), lambda b, p, c: (prev_blk(b, p, c), 1)),
            pl.BlockSpec((HALO, CONV_CH), lambda b, p, c: (next_blk(b, p, c), 1)),
            pl.BlockSpec((rows, LANES), lambda b, p, c: (row_blk(b, p, c), 0)),
            const((8, CONV_CH)), const((1, CONV_CH)), const((1, LANES)), const((1, LANES)),
            const((1, SSD_INNER)), const((1, SSD_INNER)),
        ],
        out_specs=pl.BlockSpec((rows, SSD_INNER), lambda b, p, c: (z_blk(b, p, c), 0)),
        out_shape=jax.ShapeDtypeStruct((t, SSD_INNER), BF16),
        scratch_shapes=[
            pltpu.VMEM((CONV_CH // LANES, rows + 2 * HALO, LANES), F32),
            pltpu.VMEM((seq, CONV_CH), F32),
            pltpu.VMEM((seq, SSD_INNER), F32),
            pltpu.VMEM((SSD_GROUPS, SSD_STATE, SSD_INNER // SSD_GROUPS), F32),
            pltpu.VMEM((2 * LANES, SSD_INNER), BF16),
        ],
        compiler_params=_cparams(("arbitrary", "arbitrary", "arbitrary")),
        name="ssd",
    )(zux, zux, zux, zux, dt, cw, cb, dtb, alog, dskip, nw)


POOL_PAD = 16
POOL_ROWS = 512


def _pool_kernel(u_ref, w_ref, sc_ref, out_ref, pad_ref, *, seq):
    zeros = jnp.zeros((POOL_PAD, POOL_GROUP), F32)
    for g in range(len(POOL_WINDOWS)):
        pad_ref[g, 0:POOL_PAD, :] = zeros
        pad_ref[g, POOL_PAD + seq:, :] = zeros
        pad_ref[g, POOL_PAD:POOL_PAD + seq, :] = u_ref[:, g * POOL_GROUP:(g + 1) * POOL_GROUP]
    for g, w in enumerate(POOL_WINDOWS):
        h = 1
        while h < w:
            for r in range(0, seq + POOL_PAD, POOL_ROWS):
                n = min(POOL_ROWS, seq + POOL_PAD - r)
                pad_ref[g, r:r + n, :] = pad_ref[g, r:r + n, :] + pad_ref[g, r + h:r + h + n, :]
            h *= 2
    for r0 in range(0, seq, POOL_ROWS):
        pos = r0 + lax.broadcasted_iota(jnp.int32, (POOL_ROWS, POOL_GROUP), 0)
        for g, w in enumerate(POOL_WINDOWS):
            cols = slice(g * POOL_GROUP, (g + 1) * POOL_GROUP)
            u = u_ref[r0:r0 + POOL_ROWS, cols]
            s = pad_ref[g, POOL_PAD + r0 - w // 2:POOL_PAD + r0 - w // 2 + POOL_ROWS, :]
            if r0 == 0 or r0 + POOL_ROWS == seq:
                cnt = jnp.minimum(pos + w // 2, seq) - jnp.maximum(pos - w // 2, 0)
                mean = s / cnt.astype(F32)
            else:
                mean = s * (1.0 / w)
            y = jnp.dot((mean - u).astype(BF16), w_ref[g], preferred_element_type=F32)
            out_ref[r0:r0 + POOL_ROWS, cols] = (y * sc_ref[:, cols]).astype(BF16)


def _pool(zux, pw, psc, *, batch, seq):
    t = batch * seq
    return pl.pallas_call(
        functools.partial(_pool_kernel, seq=seq),
        scratch_shapes=[pltpu.VMEM((len(POOL_WINDOWS), seq + 2 * POOL_PAD, POOL_GROUP), F32)],
        grid=(batch,),
        in_specs=[
            pl.BlockSpec((seq, POOL_WIDTH), lambda b: (b, SSD_INNER // POOL_WIDTH)),
            pl.BlockSpec((len(POOL_WINDOWS), POOL_GROUP, POOL_GROUP), lambda b: (0, 0, 0)),
            pl.BlockSpec((1, POOL_WIDTH), lambda b: (0, 0)),
        ],
        out_specs=pl.BlockSpec((seq, POOL_WIDTH), lambda b: (b, 0)),
        out_shape=jax.ShapeDtypeStruct((t, POOL_WIDTH), BF16),
        compiler_params=_cparams(("arbitrary",)),
        name="pool",
    )(zux, pw, psc)


def _outproj_kernel(x_ref, o0_ref, l0_ref, o1_ref, l1_ref, o2_ref, l2_ref, s_ref, p_ref, w_ref, o_ref,
                    so_ref, sl_ref, *, tm):
    s0, p0 = ATT_OUT, ATT_OUT + SSD_INNER
    att = _blend_groups(((o0_ref, l0_ref), (o1_ref, l1_ref), (o2_ref, l2_ref)), so_ref, sl_ref, tm)
    acc = jnp.dot(att, w_ref[0, 0:s0, :], preferred_element_type=F32)
    acc = acc + jnp.dot(s_ref[...], w_ref[0, s0:p0, :], preferred_element_type=F32)
    acc = acc + jnp.dot(p_ref[...], w_ref[0, p0:, :], preferred_element_type=F32)
    o_ref[...] = x_ref[...] + acc


def _outproj(x2, ol, ssd, pool, wo, layer, *, batch, seq, tm):
    t = x2.shape[0]
    per_b = seq // tm
    in_specs = [pl.BlockSpec((tm, D_MODEL), lambda i: (i, 0))]
    args = [x2]
    for d, (o, l) in zip(DILATIONS, ol):
        spec = pl.BlockSpec((1, d, tm // d, ATT_OUT), lambda i: (i // per_b, 0, i % per_b, 0))
        in_specs += [spec, spec]
        args += [o, l]
    in_specs += [
        pl.BlockSpec((tm, SSD_INNER), lambda i: (i, 0)),
        pl.BlockSpec((tm, POOL_WIDTH), lambda i: (i, 0)),
        pl.BlockSpec((1, MIX_WIDTH, D_MODEL), lambda i: (layer, 0, 0)),
    ]
    return pl.pallas_call(
        functools.partial(_outproj_kernel, tm=tm),
        grid=(t // tm,),
        in_specs=in_specs,
        out_specs=pl.BlockSpec((tm, D_MODEL), lambda i: (i, 0)),
        out_shape=jax.ShapeDtypeStruct((t, D_MODEL), F32),
        scratch_shapes=[pltpu.VMEM((ATT_OUT // LANES, tm, LANES), F32)] * 2,
        compiler_params=_cparams(("arbitrary",)),
        name="outproj",
    )(*args, ssd, pool, wo)


def _ffn_kernel(x_ref, nw_ref, wg_ref, wu_ref, wd_ref, fw_ref, o_ref, h_ref, *, n_ff, final):
    j = pl.program_id(1)

    @pl.when(j == 0)
    def _():
        x = x_ref[...]
        ms = jnp.mean(x * x, axis=-1, keepdims=True)
        h_ref[...] = (x * lax.rsqrt(ms + RMS_EPS) * nw_ref[...]).astype(BF16)
        o_ref[...] = x

    h = h_ref[...]
    gate = jnp.dot(h, wg_ref[0], preferred_element_type=F32)
    up = jnp.dot(h, wu_ref[0], preferred_element_type=F32)
    act = (_silu(gate) * up).astype(BF16)
    o_ref[...] += jnp.dot(act, wd_ref[0], preferred_element_type=F32)

    if final:
        @pl.when(j == n_ff - 1)
        def _():
            y = o_ref[...]
            ms = jnp.mean(y * y, axis=-1, keepdims=True)
            o_ref[...] = y * lax.rsqrt(ms + RMS_EPS) * fw_ref[...]


def _ffn(x2, nw, wg, wu, wd, fw, layer, *, tm, tf, final):
    t = x2.shape[0]
    n_ff = D_FF // tf
    return pl.pallas_call(
        functools.partial(_ffn_kernel, n_ff=n_ff, final=final),
        grid=(t // tm, n_ff),
        in_specs=[
            pl.BlockSpec((tm, D_MODEL), lambda i, j: (i, 0)),
            pl.BlockSpec((1, D_MODEL), lambda i, j: (0, 0)),
            pl.BlockSpec((1, D_MODEL, tf), lambda i, j: (layer, 0, j)),
            pl.BlockSpec((1, D_MODEL, tf), lambda i, j: (layer, 0, j)),
            pl.BlockSpec((1, tf, D_MODEL), lambda i, j: (layer, j, 0)),
            pl.BlockSpec((1, D_MODEL), lambda i, j: (0, 0)),
        ],
        out_specs=pl.BlockSpec((tm, D_MODEL), lambda i, j: (i, 0)),
        out_shape=jax.ShapeDtypeStruct((t, D_MODEL), F32),
        scratch_shapes=[pltpu.VMEM((tm, D_MODEL), BF16)],
        compiler_params=_cparams(("arbitrary", "arbitrary")),
        name="ffn_final" if final else "ffn",
    )(x2, nw, wg, wu, wd, fw)


PACK_W = 512
PACK_U_TILE = 11


def _pack_kernel(wt_ref, tail_ref, o_ref, wdt_ref):
    j = pl.program_id(0)
    n_dt = 2 * SSD_HEADS

    @pl.when(j == 0)
    def _():
        lane = lax.broadcasted_iota(jnp.int32, (D_MODEL, LANES), 1)
        wdt_ref[...] = jnp.where(lane < n_dt, tail_ref[:, 0:LANES], 0.0).astype(BF16)

    @pl.when(j != PACK_U_TILE)
    def _():
        o_ref[...] = wt_ref[0].T.astype(BF16)

    @pl.when(j == PACK_U_TILE)
    def _():
        o_ref[...] = tail_ref[:, n_dt:n_dt + POOL_WIDTH].astype(BF16)


def _pack_in_weights(w_in, w_in_t, layer):
    z0 = 3 * ATT_WIDTH
    dt0 = z0 + SSD_INNER + CONV_CH
    n_tail = w_in.shape[2] - dt0
    n_qkv = z0 // PACK_W

    def src_tile(j):
        qkv = (j % N_DIL) * 3 + j // N_DIL
        return jnp.where(j < n_qkv, qkv, jnp.where(j < PACK_U_TILE, j, j - 1))

    return pl.pallas_call(
        _pack_kernel,
        grid=(N_IN_TILES * IN_TILE // PACK_W,),
        in_specs=[
            pl.BlockSpec((1, PACK_W, D_MODEL), lambda j: (layer, src_tile(j), 0)),
            pl.BlockSpec((D_MODEL, n_tail), lambda j: (0, 0)),
        ],
        out_specs=[pl.BlockSpec((D_MODEL, PACK_W), lambda j: (0, j)),
                   pl.BlockSpec((D_MODEL, LANES), lambda j: (0, 0))],
        out_shape=[jax.ShapeDtypeStruct((D_MODEL, N_IN_TILES * IN_TILE), BF16),
                   jax.ShapeDtypeStruct((D_MODEL, LANES), BF16)],
        compiler_params=_cparams(("arbitrary",)),
        name="pack_w_in",
    )(w_in_t, w_in[layer, :, dt0:])


def _pad_lanes(v):
    return jnp.pad(v.reshape(1, -1).astype(F32), ((0, 0), (0, LANES - v.size)))


def _layer(x2, slopes, p, fw, *, batch, seq, final):
    wcat, wdt = _pack_in_weights(p["w_in"], p["w_in_t"], p["layer"])
    q0, q1, q2, zux, dt = _inproj(x2, p["norm1_w"].reshape(1, -1), wcat, wdt,
                                  batch=batch, seq=seq, tm=IN_TM)
    ol = [_attention_group(qkv, slopes[g], dil=d) for g, (d, qkv) in enumerate(zip(DILATIONS, (q0, q1, q2)))]
    cw = jnp.pad(p["conv_w"].astype(F32), ((0, 8 - SSD_CONV), (0, 0)))
    ssd = _ssd(zux, dt, cw, p["conv_b"].reshape(1, -1).astype(F32), _pad_lanes(p["dt_bias"]),
               _pad_lanes(p["a_log"]), jnp.repeat(p["d_skip"].astype(F32), SSD_HEAD_DIM).reshape(1, -1),
               p["ssd_norm_w"].reshape(1, -1).astype(F32), batch=batch, seq=seq)
    pool = _pool(zux, p["pool_w"].astype(BF16), p["pool_scale"].reshape(1, -1).astype(F32),
                 batch=batch, seq=seq)
    x2 = _outproj(x2, ol, ssd, pool, p["w_out"], p["layer"], batch=batch, seq=seq, tm=OUT_TM)
    return _ffn(x2, p["norm2_w"].reshape(1, -1), p["w_gate"], p["w_up"], p["w_down"], fw.reshape(1, -1),
                p["layer"], tm=FFN_TM, tf=FFN_TF, final=final)


def kernel(x, norm1_w, w_in, conv_w, conv_b, dt_bias, a_log, d_skip, ssd_norm_w, pool_w, pool_scale,
           w_out, norm2_w, w_gate, w_up, w_down, final_norm_w):
    batch, seq, _ = x.shape
    depth = w_in.shape[0]
    k = jnp.arange(1, ATT_HEADS + 1, dtype=F32)
    slopes = (2.0 ** (-8.0 * k / ATT_HEADS)).reshape(N_DIL, ATT_SLOTS)
    x2 = x.reshape(batch * seq, D_MODEL)
    wo, wg, wu, wd = (w.astype(BF16) for w in (w_out, w_gate, w_up, w_down))
    w_in_t = jnp.swapaxes(w_in, 1, 2)
    for l in range(depth):
        p = dict(norm1_w=norm1_w[l], w_in=w_in, w_in_t=w_in_t, layer=l, conv_w=conv_w[l], conv_b=conv_b[l], dt_bias=dt_bias[l],
                 a_log=a_log[l], d_skip=d_skip[l], ssd_norm_w=ssd_norm_w[l], pool_w=pool_w[l],
                 pool_scale=pool_scale[l], w_out=wo, norm2_w=norm2_w[l], w_gate=wg, w_up=wu, w_down=wd)
        x2 = _layer(x2, slopes, p, final_norm_w, batch=batch, seq=seq, final=(l == depth - 1))
    return x2.reshape(batch, seq, D_MODEL)
```
